```python
import jax, jax.numpy as jnp
from jax import lax
import numpy as np

D_MODEL = 1024
BATCH = 1
SEQ = 16384
DEPTH = 2
DEC_BATCH = 128
DEC_SEQ = 4
PAST_LEN = 16384
PAGE_SIZE = 128

MLA_WIDTH = D_MODEL // 2
CONV_WIDTH = D_MODEL // 4
MEM_WIDTH = D_MODEL // 4
MIX_WIDTH = MLA_WIDTH + CONV_WIDTH + MEM_WIDTH
N_HEADS = 8
V_HEAD = MLA_WIDTH // N_HEADS
QK_NOPE = 64
QK_ROPE = 32
Q_RANK = (3 * D_MODEL) // 8
KV_RANK = D_MODEL // 4
ROPE_THETA = 10000.0
Q_BLOCK = 128
MLA_SCALE = (QK_NOPE + QK_ROPE) ** -0.5
CONV_K = 31
MEM_TOKENS = 256
MEM_HEADS = 4
MEM_HEAD_DIM = MEM_WIDTH // MEM_HEADS
MEM_SCALE = MEM_HEAD_DIM ** -0.5
N_GROUPS = 4
EXPERTS_PER_GROUP = 8
N_EXPERTS = N_GROUPS * EXPERTS_PER_GROUP
TOP_K = 2
D_EXPERT = D_MODEL // 4
MOE_BLOCK = 1024
EPS = 1e-6
IN_COLS = Q_RANK + KV_RANK + QK_ROPE + 2 * CONV_WIDTH + MEM_WIDTH

kernel_name = 'hymba_mla_conformer_hmoe_step'


def rms_norm(x, g):
    xf = x.astype(jnp.float32)
    y = xf * lax.rsqrt(jnp.mean(xf * xf, axis=-1, keepdims=True) + EPS)
    return (y * g).astype(x.dtype)


def layer_norm(x, g, b):
    xf = x.astype(jnp.float32)
    mu = jnp.mean(xf, axis=-1, keepdims=True)
    var = jnp.mean(jnp.square(xf - mu), axis=-1, keepdims=True)
    return ((xf - mu) * lax.rsqrt(var + EPS) * g + b).astype(x.dtype)


def rope_tables(pos):
    inv_freq = 1.0 / (ROPE_THETA ** (jnp.arange(0, QK_ROPE, 2, dtype=jnp.float32) / QK_ROPE))
    ang = pos.astype(jnp.float32)[:, None] * inv_freq[None, :]
    return jnp.cos(ang), jnp.sin(ang)


def apply_rope(x, cos, sin):
    half = QK_ROPE // 2
    x1 = x[..., :half].astype(jnp.float32)
    x2 = x[..., half:].astype(jnp.float32)
    return jnp.concatenate([x1 * cos - x2 * sin, x1 * sin + x2 * cos], axis=-1).astype(x.dtype)


def split_in(z):
    cuts = np.cumsum([Q_RANK, KV_RANK, QK_ROPE, CONV_WIDTH, CONV_WIDTH])
    return jnp.split(z, [int(c) for c in cuts], axis=-1)


def mla_query(c_q, p, cos, sin):
    q = rms_norm(c_q, p['q_a_norm_g']) @ p['w_uq']
    q = q.reshape(q.shape[:-1] + (N_HEADS, QK_NOPE + QK_ROPE))
    q_nope = rms_norm(q[..., :QK_NOPE], p['qn_nope_g'])
    q_rope = apply_rope(rms_norm(q[..., QK_NOPE:], p['qn_rope_g']), cos[:, None, :], sin[:, None, :])
    return jnp.concatenate([q_nope, q_rope], axis=-1)


def mla_latent(c_kv, k_rope, p, cos, sin):
    return rms_norm(c_kv, p['kv_a_norm_g']), apply_rope(rms_norm(k_rope, p['kn_rope_g']), cos, sin)


def mla_expand(ckv, kpe, p):
    kv = (ckv @ p['w_ukv']).reshape(ckv.shape[:-1] + (N_HEADS, QK_NOPE + V_HEAD))
    k_nope = rms_norm(kv[..., :QK_NOPE], p['kn_nope_g'])
    k_pe = jnp.broadcast_to(kpe[..., None, :], k_nope.shape[:-1] + (QK_ROPE,))
    return jnp.concatenate([k_nope, k_pe], axis=-1), kv[..., QK_NOPE:]


def mla_causal_attention(q, k, v):
    b, s = q.shape[:2]
    nb = s // Q_BLOCK
    q_blocks = jnp.moveaxis(q.reshape(b, nb, Q_BLOCK, N_HEADS, QK_NOPE + QK_ROPE), 1, 0)
    k_pos = jnp.arange(s)

    def one_block(args):
        q_blk, blk = args
        q_pos = blk * Q_BLOCK + jnp.arange(Q_BLOCK)
        sc = jnp.einsum('bqhd,bkhd->bhqk', q_blk, k).astype(jnp.float32) * MLA_SCALE
        sc = jnp.where(k_pos[None, :] <= q_pos[:, None], sc, -jnp.inf)
        pr = jax.nn.softmax(sc, axis=-1).astype(v.dtype)
        return jnp.einsum('bhqk,bkhd->bqhd', pr, v)

    o = lax.map(one_block, (q_blocks, jnp.arange(nb)))
    return jnp.moveaxis(o, 0, 1).reshape(b, s, MLA_WIDTH)


def mla_paged_attention(q, k_new, v_new, cache_ckv, cache_kpe, layer, page_table, p):
    n_new = q.shape[1]
    causal = jnp.tril(jnp.ones((n_new, n_new), dtype=bool))

    def one_sequence(args):
        pages, q_i, kn_i, vn_i = args
        ckv = cache_ckv[layer, pages].reshape(-1, KV_RANK)
        kpe = cache_kpe[layer, pages].reshape(-1, QK_ROPE)
        k_past, v_past = mla_expand(ckv, kpe, p)
        n_past = k_past.shape[0]
        s_past = jnp.einsum('qhd,khd->hqk', q_i, k_past).astype(jnp.float32)
        s_new = jnp.where(causal, jnp.einsum('qhd,khd->hqk', q_i, kn_i).astype(jnp.float32), -jnp.inf)
        pr = jax.nn.softmax(jnp.concatenate([s_past, s_new], axis=-1) * MLA_SCALE, axis=-1).astype(v_past.dtype)
        return (jnp.einsum('hqk,khd->qhd', pr[..., :n_past], v_past)
                + jnp.einsum('hqk,khd->qhd', pr[..., n_past:], vn_i))

    o = lax.map(one_sequence, (page_table, q, k_new, v_new))
    return o.reshape(o.shape[:2] + (MLA_WIDTH,))


def conformer_conv(u_val, u_gate, left, p):
    glu = u_val * jax.nn.sigmoid(u_gate)
    padded = jnp.concatenate([left.astype(glu.dtype), glu], axis=1)
    y = lax.conv_general_dilated(padded, p['conv_w'][:, None, :], window_strides=(1,), padding='VALID',
                                 dimension_numbers=('NWC', 'WIO', 'NWC'),
                                 feature_group_count=CONV_WIDTH) + p['conv_b']
    y = layer_norm(y, p['conv_ln_g'], p['conv_ln_b'])
    return y * jax.nn.sigmoid(y), padded[:, -(CONV_K - 1):]


def memory_keys_values(mem, p):
    m = rms_norm(mem, p['mem_norm_g'])
    shp = m.shape[:-1] + (MEM_HEADS, MEM_HEAD_DIM)
    k = rms_norm((m @ p['w_mem_k']).reshape(shp), p['mem_kn_g'])
    v = (m @ p['w_mem_v']).reshape(shp)
    return k, v


def memory_attention(q_mem, mem_k, mem_v, p):
    q = rms_norm(q_mem.reshape(q_mem.shape[:-1] + (MEM_HEADS, MEM_HEAD_DIM)), p['mem_qn_g'])
    sc = jnp.einsum('bshd,bmhd->bhsm', q, mem_k).astype(jnp.float32) * MEM_SCALE
    pr = jax.nn.softmax(sc, axis=-1).astype(mem_v.dtype)
    o = jnp.einsum('bhsm,bmhd->bshd', pr, mem_v)
    return o.reshape(o.shape[:2] + (MEM_WIDTH,))


def mixer_sublayer(x, p, cos, sin, mla_attend, conv_left, mem_k, mem_v):
    h = rms_norm(x, p['norm1_g'])
    c_q, c_kv, k_rope, u_val, u_gate, q_mem = split_in(h @ p['w_in'])
    q = mla_query(c_q, p, cos, sin)
    ckv, kpe = mla_latent(c_kv, k_rope, p, cos, sin)
    a_mla = mla_attend(q, ckv, kpe)
    a_conv, conv_state = conformer_conv(u_val, u_gate, conv_left, p)
    a_mem = memory_attention(q_mem, mem_k, mem_v, p)
    y = jnp.concatenate([a_mla, a_conv, a_mem], axis=-1) @ p['w_out']
    return x + y, ckv, kpe, conv_state


def hier_moe(x, p):
    shp = x.shape
    t = x.reshape(-1, D_MODEL)
    n_tok = t.shape[0]
    blk = min(MOE_BLOCK, n_tok)
    n_blk = -(-n_tok // blk)
    t = jnp.pad(t, ((0, n_blk * blk - n_tok), (0, 0)))

    def one_block(tb):
        gp = jax.nn.softmax((tb @ p['w_router_group'] + p['b_router_group']).astype(jnp.float32), axis=-1)
        p_grp, grp = lax.top_k(gp, 1)
        el = (tb @ p['w_router_expert'] + p['b_router_expert']).astype(jnp.float32)
        el = el.reshape(blk, N_GROUPS, EXPERTS_PER_GROUP)
        el = jnp.einsum('tge,tg->te', el, jax.nn.one_hot(grp[:, 0], N_GROUPS, dtype=jnp.float32))
        p_top, i_top = lax.top_k(jax.nn.softmax(el, axis=-1), TOP_K)
        w = p_top / jnp.sum(p_top, axis=-1, keepdims=True) * p_grp
        idx = grp * EXPERTS_PER_GROUP + i_top
        gate = jnp.sum(jax.nn.one_hot(idx, N_EXPERTS, dtype=jnp.float32) * w[..., None], axis=1)
        h = jax.nn.silu(jnp.einsum('td,edf->tef', tb, p['w_gate'])) * jnp.einsum('td,edf->tef', tb, p['w_up'])
        h = h * gate[..., None].astype(h.dtype)
        return jnp.einsum('tef,efd->td', h, p['w_down'])

    y = lax.map(one_block, t.reshape(n_blk, blk, D_MODEL))
    return y.reshape(n_blk * blk, D_MODEL)[:n_tok].reshape(shp)


def setup_inputs(seed: int = 0) -> dict:
    key = jax.random.key(seed)
    ks = iter(jax.random.split(key, 48))

    def nrm(shape, scale):
        return jax.random.normal(next(ks), shape, jnp.float32) * scale

    def gain(shape):
        return 1.0 + 0.02 * jax.random.normal(next(ks), shape, jnp.float32)

    n_pages = PAST_LEN // PAGE_SIZE
    n_used = DEC_BATCH * n_pages
    n_pool = n_used + max(1, n_used // 4)
    L = DEPTH
    x_prompt = nrm((BATCH, SEQ, D_MODEL), 1.0)
    x_sample = nrm((DEC_BATCH, DEC_SEQ, D_MODEL), 1.0)
    cache_ckv = nrm((L, n_pool, PAGE_SIZE, KV_RANK), 1.0)
    cache_kpe = nrm((L, n_pool, PAGE_SIZE, QK_ROPE), 1.0)
    state_conv = nrm((L, DEC_BATCH, CONV_K - 1, CONV_WIDTH), 0.5)
    cache_mem_k = nrm((L, DEC_BATCH, MEM_TOKENS, MEM_HEADS, MEM_HEAD_DIM), 1.0)
    cache_mem_v = nrm((L, DEC_BATCH, MEM_TOKENS, MEM_HEADS, MEM_HEAD_DIM), 1.0)
    page_table = jax.random.permutation(next(ks), n_pool)[:n_used].reshape(DEC_BATCH, n_pages).astype(jnp.int32)
    mem_prompt = nrm((BATCH, MEM_TOKENS, D_MODEL), 1.0)
    return {
        'x_prompt': x_prompt, 'x_sample': x_sample,
        'cache_ckv': cache_ckv, 'cache_kpe': cache_kpe, 'state_conv': state_conv,
        'cache_mem_k': cache_mem_k, 'cache_mem_v': cache_mem_v, 'page_table': page_table,
        'mem_prompt': mem_prompt,
        'norm1_g': gain((L, D_MODEL)),
        'w_in': nrm((L, D_MODEL, IN_COLS), D_MODEL ** -0.5),
        'q_a_norm_g': gain((L, Q_RANK)),
        'w_uq': nrm((L, Q_RANK, N_HEADS * (QK_NOPE + QK_ROPE)), Q_RANK ** -0.5),
        'kv_a_norm_g': gain((L, KV_RANK)),
        'w_ukv': nrm((L, KV_RANK, N_HEADS * (QK_NOPE + V_HEAD)), KV_RANK ** -0.5),
        'qn_nope_g': gain((L, QK_NOPE)), 'qn_rope_g': gain((L, QK_ROPE)),
        'kn_nope_g': gain((L, QK_NOPE)), 'kn_rope_g': gain((L, QK_ROPE)),
        'conv_w': nrm((L, CONV_K, CONV_WIDTH), CONV_K ** -0.5),
        'conv_b': nrm((L, CONV_WIDTH), 0.01),
        'conv_ln_g': gain((L, CONV_WIDTH)), 'conv_ln_b': nrm((L, CONV_WIDTH), 0.01),
        'mem_norm_g': gain((L, D_MODEL)),
        'w_mem_k': nrm((L, D_MODEL, MEM_WIDTH), D_MODEL ** -0.5),
        'w_mem_v': nrm((L, D_MODEL, MEM_WIDTH), D_MODEL ** -0.5),
        'mem_qn_g': gain((L, MEM_HEAD_DIM)), 'mem_kn_g': gain((L, MEM_HEAD_DIM)),
        'w_out': nrm((L, MIX_WIDTH, D_MODEL), MIX_WIDTH ** -0.5),
        'norm2_g': gain((L, D_MODEL)),
        'w_router_group': nrm((L, D_MODEL, N_GROUPS), D_MODEL ** -0.5),
        'b_router_group': nrm((L, N_GROUPS), 0.01),
        'w_router_expert': nrm((L, D_MODEL, N_EXPERTS), D_MODEL ** -0.5),
        'b_router_expert': nrm((L, N_EXPERTS), 0.01),
        'w_gate': nrm((L, N_EXPERTS, D_MODEL, D_EXPERT), D_MODEL ** -0.5),
        'w_up': nrm((L, N_EXPERTS, D_MODEL, D_EXPERT), D_MODEL ** -0.5),
        'w_down': nrm((L, N_EXPERTS, D_EXPERT, D_MODEL), D_EXPERT ** -0.5),
    }


def reference(x_prompt, x_sample, cache_ckv, cache_kpe, state_conv, cache_mem_k, cache_mem_v, page_table,
              mem_prompt, norm1_g, w_in, q_a_norm_g, w_uq, kv_a_norm_g, w_ukv, qn_nope_g, qn_rope_g,
              kn_nope_g, kn_rope_g, conv_w, conv_b, conv_ln_g, conv_ln_b, mem_norm_g, w_mem_k, w_mem_v,
              mem_qn_g, mem_kn_g, w_out, norm2_g, w_router_group, b_router_group, w_router_expert,
              b_router_expert, w_gate, w_up, w_down):
    n_seq = x_prompt.shape[1]
    n_dec = x_sample.shape[1]
    past_len = page_table.shape[1] * PAGE_SIZE
    cos_p, sin_p = rope_tables(jnp.arange(n_seq))
    cos_s, sin_s = rope_tables(past_len + jnp.arange(n_dec))
    xp, xs = x_prompt, x_sample
    ckv_p_l, kpe_p_l, conv_p_l, memk_p_l, memv_p_l = [], [], [], [], []
    ckv_s_l, kpe_s_l, conv_s_l = [], [], []
    for l in range(DEPTH):
        p = {
            'norm1_g': norm1_g[l], 'w_in': w_in[l], 'q_a_norm_g': q_a_norm_g[l], 'w_uq': w_uq[l],
            'kv_a_norm_g': kv_a_norm_g[l], 'w_ukv': w_ukv[l], 'qn_nope_g': qn_nope_g[l],
            'qn_rope_g': qn_rope_g[l], 'kn_nope_g': kn_nope_g[l], 'kn_rope_g': kn_rope_g[l],
            'conv_w': conv_w[l], 'conv_b': conv_b[l], 'conv_ln_g': conv_ln_g[l], 'conv_ln_b': conv_ln_b[l],
            'mem_norm_g': mem_norm_g[l], 'w_mem_k': w_mem_k[l], 'w_mem_v': w_mem_v[l],
            'mem_qn_g': mem_qn_g[l], 'mem_kn_g': mem_kn_g[l], 'w_out': w_out[l], 'norm2_g': norm2_g[l],
            'w_router_group': w_router_group[l], 'b_router_group': b_router_group[l],
            'w_router_expert': w_router_expert[l], 'b_router_expert': b_router_expert[l],
            'w_gate': w_gate[l], 'w_up': w_up[l], 'w_down': w_down[l],
        }

        def attend_prompt(q, ckv, kpe, p=p):
            k, v = mla_expand(ckv, kpe, p)
            return mla_causal_attention(q, k, v)

        def attend_sample(q, ckv, kpe, p=p, l=l):
            k_new, v_new = mla_expand(ckv, kpe, p)
            return mla_paged_attention(q, k_new, v_new, cache_ckv, cache_kpe, l, page_table, p)

        mk_p, mv_p = memory_keys_values(mem_prompt, p)
        zero_left = jnp.zeros((xp.shape[0], CONV_K - 1, CONV_WIDTH), xp.dtype)
        xp, ckv_p, kpe_p, conv_p = mixer_sublayer(xp, p, cos_p, sin_p, attend_prompt, zero_left, mk_p, mv_p)
        xp = xp + hier_moe(rms_norm(xp, p['norm2_g']), p)
        xs, ckv_s, kpe_s, conv_s = mixer_sublayer(xs, p, cos_s, sin_s, attend_sample, state_conv[l],
                                                  cache_mem_k[l], cache_mem_v[l])
        xs = xs + hier_moe(rms_norm(xs, p['norm2_g']), p)

        ckv_p_l.append(ckv_p); kpe_p_l.append(kpe_p); conv_p_l.append(conv_p)
        memk_p_l.append(mk_p); memv_p_l.append(mv_p)
        ckv_s_l.append(ckv_s); kpe_s_l.append(kpe_s); conv_s_l.append(conv_s)
    return (xp, xs, jnp.stack(ckv_p_l), jnp.stack(kpe_p_l), jnp.stack(conv_p_l), jnp.stack(memk_p_l),
            jnp.stack(memv_p_l), jnp.stack(ckv_s_l), jnp.stack(kpe_s_l), jnp.stack(conv_s_l))
```

```python
import functools

import numpy as np
import jax
import jax.numpy as jnp
from jax import lax
from jax.experimental import pallas as pl
from jax.experimental.pallas import tpu as pltpu

D_MODEL = 1024
N_HEADS = 8
QK_NOPE = 64
QK_ROPE = 32
V_HEAD = 64
Q_RANK = 384
KV_RANK = 256
MLA_WIDTH = N_HEADS * V_HEAD
CONV_WIDTH = 256
MEM_WIDTH = 256
CONV_K = 31
MEM_HEADS = 4
MEM_HEAD_DIM = 64
N_GROUPS = 4
EXPERTS_PER_GROUP = 8
N_EXPERTS = 32
D_EXPERT = 256
PAGE_SIZE = 128
ROPE_THETA = 10000.0
EPS = 1e-6
MLA_SCALE = (QK_NOPE + QK_ROPE) ** -0.5
MEM_SCALE = MEM_HEAD_DIM ** -0.5

LANES = 128
SUBLANES = 8
VMEM_BYTES = 64 << 20
HEAD_PAD = LANES

_CQ0, _CKV0, _UV0, _UG0, _QM0, _KR0, _IN_PAD = 0, 384, 640, 896, 1152, 1408, 1536
_ROPE0 = QK_NOPE
_HALF = QK_ROPE // 2
NEG = -1e30

BF = jnp.bfloat16
F32 = jnp.float32


def _vmem_limit(nbytes):
    return int(min(max(2 * nbytes, 16 << 20), VMEM_BYTES - (8 << 20)))


def _dot(a, b):
    return jnp.dot(a, b, preferred_element_type=F32)


def _dot_nt(a, b):
    return lax.dot_general(a, b, (((1,), (1,)), ((), ())), preferred_element_type=F32)


def _rms(x, g):
    return x * lax.rsqrt(jnp.mean(x * x, axis=-1, keepdims=True) + EPS) * g


def _sigmoid(x):
    return 1.0 / (1.0 + jnp.exp(-x))


def _rope_padded(x, c, s):
    lane = lax.broadcasted_iota(jnp.int32, x.shape, 1)
    swapped = jnp.where(lane < _ROPE0 + _HALF, pltpu.roll(x, LANES - _HALF, 1), pltpu.roll(x, _HALF, 1))
    return x * c + swapped * s


def _premix_kernel(x_ref, n1g_ref, win_ref, qag_ref, wuq_ref, kvg_ref, wuk_ref, wuv_ref, gq_ref, gk_ref,
                   gkr_ref, grp_q_ref, grp_k_ref, grp_m_ref, mqg_ref, cos_ref, sin_ref,
                   q_out, k_out, v_out, ckv_out, kpe_out, glu_out, qm_out):
    x = x_ref[...]
    h = _rms(x, n1g_ref[...]).astype(BF)

    def seg(a, b):
        return _dot(h, win_ref[:, a:b])

    c = cos_ref[...]
    s = sin_ref[...]

    cq = _rms(seg(_CQ0, _CKV0), qag_ref[...]).astype(BF)
    q = _dot(cq, wuq_ref[...])
    q = q * lax.rsqrt(_dot((q * q).astype(BF), grp_q_ref[...]) + EPS)
    gq = gq_ref[...]
    for hd in range(N_HEADS):
        qh = q[:, hd * HEAD_PAD:(hd + 1) * HEAD_PAD] * gq
        q_out[hd] = _rope_padded(qh, c, s).astype(BF)

    ckv = _rms(seg(_CKV0, _UV0), kvg_ref[...])
    ckv_out[...] = ckv
    ckvb = ckv.astype(BF)
    kr = seg(_KR0, _IN_PAD)
    kr = kr * lax.rsqrt(jnp.sum(kr * kr, axis=-1, keepdims=True) * (1.0 / QK_ROPE) + EPS) * gkr_ref[...]
    kpe = _rope_padded(kr, c, s)
    kpe_out[...] = kpe[:, _ROPE0:_ROPE0 + QK_ROPE]

    kx = _dot(ckvb, wuk_ref[...])
    kx = kx * lax.rsqrt(_dot((kx * kx).astype(BF), grp_k_ref[...]) + EPS)
    vx = _dot(ckvb, wuv_ref[...])
    gk = gk_ref[...]
    for hd in range(N_HEADS):
        sl = slice(hd * HEAD_PAD, (hd + 1) * HEAD_PAD)
        k_out[hd] = (kx[:, sl] * gk + kpe).astype(BF)
        v_out[hd] = vx[:, sl].astype(BF)

    glu_out[...] = seg(_UV0, _UG0) * _sigmoid(seg(_UG0, _QM0))
    qm = seg(_QM0, _KR0)
    qm = qm * lax.rsqrt(_dot((qm * qm).astype(BF), grp_m_ref[...]) + EPS) * mqg_ref[...]
    qm_out[...] = qm.astype(BF)


def _premix(x, w, cos_t, sin_t, tm):
    t = x.shape[0]
    nt = t // tm
    row = lambda i: (i, 0)
    full = lambda i: (0, 0)
    hm = lambda i: (0, i, 0)
    wspec = lambda a: pl.BlockSpec(a.shape, full)
    ins = [x, w['n1g'], w['win'], w['qag'], w['wuq'], w['kvg'], w['wuk'], w['wuv'], w['gq'], w['gk'], w['gkr'],
           w['grp_q'], w['grp_k'], w['grp_m'], w['mqg'], cos_t, sin_t]
    in_specs = [pl.BlockSpec((tm, D_MODEL), row)] + [wspec(a) for a in ins[1:15]] + \
               [pl.BlockSpec((tm, LANES), row), pl.BlockSpec((tm, LANES), row)]
    hshape = jax.ShapeDtypeStruct((N_HEADS, t, HEAD_PAD), BF)
    hspec = pl.BlockSpec((N_HEADS, tm, HEAD_PAD), hm)
    out_shape = (hshape, hshape, hshape,
                 jax.ShapeDtypeStruct((t, KV_RANK), F32), jax.ShapeDtypeStruct((t, QK_ROPE), F32),
                 jax.ShapeDtypeStruct((t, CONV_WIDTH), F32), jax.ShapeDtypeStruct((t, MEM_WIDTH), BF))
    out_specs = (hspec, hspec, hspec, pl.BlockSpec((tm, KV_RANK), row), pl.BlockSpec((tm, QK_ROPE), row),
                 pl.BlockSpec((tm, CONV_WIDTH), row), pl.BlockSpec((tm, MEM_WIDTH), row))
    wbytes = sum(int(np.prod(a.shape)) * a.dtype.itemsize for a in ins[1:15])
    est = 2 * wbytes + tm * (2 * 4 * D_MODEL + 3 * 2 * 2 * N_HEADS * HEAD_PAD + 8 * 4 * N_HEADS * HEAD_PAD)
    return pl.pallas_call(
        _premix_kernel, grid=(nt,), in_specs=in_specs, out_specs=out_specs, out_shape=out_shape,
        compiler_params=pltpu.CompilerParams(dimension_semantics=("arbitrary",),
                                             vmem_limit_bytes=_vmem_limit(est)),
        name="premix")(*ins)


def _flash_kernel(qi_ref, kj_ref, q_ref, k_ref, v_ref, o_ref, m_scr, l_scr, acc_scr, *, tq):
    p = pl.program_id(0)
    i = qi_ref[p]
    j = kj_ref[p]

    @pl.when(j == 0)
    def _():
        m_scr[...] = jnp.full(m_scr.shape, NEG, F32)
        l_scr[...] = jnp.zeros(l_scr.shape, F32)
        acc_scr[...] = jnp.zeros(acc_scr.shape, F32)

    def sweep(masked):
        def body(hd, carry):
            s = _dot_nt(q_ref[hd], k_ref[hd])
            if masked:
                r = lax.broadcasted_iota(jnp.int32, s.shape, 0)
                cidx = lax.broadcasted_iota(jnp.int32, s.shape, 1)
                s = jnp.where(cidx <= r, s, NEG)
            m_prev = m_scr[hd]
            m_new = jnp.maximum(m_prev, jnp.max(s, axis=1, keepdims=True))
            alpha = jnp.exp(m_prev - m_new)
            pr = jnp.exp(s - m_new[:, :1])
            l_scr[hd] = alpha * l_scr[hd] + jnp.sum(pr, axis=1, keepdims=True)
            acc_scr[hd] = alpha * acc_scr[hd] + _dot(pr.astype(BF), v_ref[hd])
            m_scr[hd] = m_new
            return carry
        lax.fori_loop(0, N_HEADS, body, 0)

    @pl.when(j < i)
    def _():
        sweep(False)

    @pl.when(j == i)
    def _():
        sweep(True)
        for hp in range(N_HEADS // 2):
            a = acc_scr[2 * hp] / l_scr[2 * hp]
            b = acc_scr[2 * hp + 1] / l_scr[2 * hp + 1]
            o_ref[:, hp * LANES:(hp + 1) * LANES] = jnp.concatenate(
                [a[:, :V_HEAD], b[:, :V_HEAD]], axis=1).astype(o_ref.dtype)


def _flash_attention(q, k, v, tq):
    t = q.shape[1]
    nq = t // tq
    qi = np.concatenate([np.full(i + 1, i, np.int32) for i in range(nq)])
    kj = np.concatenate([np.arange(i + 1, dtype=np.int32) for i in range(nq)])
    blk = (N_HEADS, tq, HEAD_PAD)
    grid_spec = pltpu.PrefetchScalarGridSpec(
        num_scalar_prefetch=2, grid=(len(qi),),
        in_specs=[pl.BlockSpec(blk, lambda p, qi, kj: (0, qi[p], 0)),
                  pl.BlockSpec(blk, lambda p, qi, kj: (0, kj[p], 0)),
                  pl.BlockSpec(blk, lambda p, qi, kj: (0, kj[p], 0))],
        out_specs=pl.BlockSpec((tq, MLA_WIDTH), lambda p, qi, kj: (qi[p], 0)),
        scratch_shapes=[pltpu.VMEM(blk, F32), pltpu.VMEM(blk, F32), pltpu.VMEM(blk, F32)])
    nblk = N_HEADS * tq * HEAD_PAD
    est = 3 * 2 * 2 * nblk + 3 * 4 * nblk + 2 * 2 * tq * MLA_WIDTH + 3 * 4 * tq * tq
    return pl.pallas_call(
        functools.partial(_flash_kernel, tq=tq), grid_spec=grid_spec,
        out_shape=jax.ShapeDtypeStruct((t, MLA_WIDTH), BF),
        compiler_params=pltpu.CompilerParams(dimension_semantics=("arbitrary",),
                                             vmem_limit_bytes=_vmem_limit(est)),
        name="flash")(jnp.asarray(qi), jnp.asarray(kj), q, k, v)


_CONV_HALO = 32
_CONV_LEAD = _CONV_HALO - (CONV_K - 1)


def _ln_swish(y, g, b):
    mu = jnp.mean(y, axis=-1, keepdims=True)
    d = y - mu
    y = d * lax.rsqrt(jnp.mean(d * d, axis=-1, keepdims=True) + EPS) * g + b
    return y * _sigmoid(y)


def _conv_rows_kernel(main_ref, halo_ref, w_ref, b_ref, g_ref, lb_ref, o_ref, buf, *, tm):
    buf[0:tm, :] = main_ref[...]
    buf[tm:tm + _CONV_HALO, :] = halo_ref[...]
    acc = jnp.zeros((tm, CONV_WIDTH), F32)
    for j in range(CONV_K):
        acc = acc + buf[_CONV_LEAD + j:_CONV_LEAD + j + tm, :] * w_ref[j:j + 1, :]
    o_ref[...] = _ln_swish(acc + b_ref[...], g_ref[...], lb_ref[...]).astype(o_ref.dtype)


def _conv_rows(padded, w, tm):
    t = padded.shape[0] - _CONV_HALO
    full = lambda i: (0, 0)
    hb = tm // _CONV_HALO
    return pl.pallas_call(
        functools.partial(_conv_rows_kernel, tm=tm), grid=(t // tm,),
        in_specs=[pl.BlockSpec((tm, CONV_WIDTH), lambda i: (i, 0)),
                  pl.BlockSpec((_CONV_HALO, CONV_WIDTH), lambda i: ((i + 1) * hb, 0)),
                  pl.BlockSpec((CONV_K, CONV_WIDTH), full)] + [pl.BlockSpec((1, CONV_WIDTH), full)] * 3,
        out_specs=pl.BlockSpec((tm, CONV_WIDTH), lambda i: (i, 0)),
        out_shape=jax.ShapeDtypeStruct((t, CONV_WIDTH), BF),
        scratch_shapes=[pltpu.VMEM((tm + _CONV_HALO, CONV_WIDTH), F32)],
        compiler_params=pltpu.CompilerParams(dimension_semantics=("arbitrary",)),
        name="conv_rows")(padded, padded, w['conv_w'], w['conv_b'], w['conv_ln_g'], w['conv_ln_b'])


def _conv_steps_kernel(p_ref, w_ref, b_ref, g_ref, lb_ref, o_ref, *, n_new):
    for t in range(n_new):
        acc = jnp.zeros(o_ref.shape[1:], F32)
        for j in range(CONV_K):
            acc = acc + p_ref[t + j] * w_ref[j:j + 1, :]
        o_ref[t] = _ln_swish(acc + b_ref[...], g_ref[...], lb_ref[...]).astype(o_ref.dtype)


def _conv_steps(padded_t, w):
    n_new = padded_t.shape[0] - (CONV_K - 1)
    nb = padded_t.shape[1]
    full2 = lambda i: (0, 0)
    full3 = lambda i: (0, 0, 0)
    return pl.pallas_call(
        functools.partial(_conv_steps_kernel, n_new=n_new), grid=(1,),
        in_specs=[pl.BlockSpec(padded_t.shape, full3), pl.BlockSpec((CONV_K, CONV_WIDTH), full2)] +
                 [pl.BlockSpec((1, CONV_WIDTH), full2)] * 3,
        out_specs=pl.BlockSpec((n_new, nb, CONV_WIDTH), full3),
        out_shape=jax.ShapeDtypeStruct((n_new, nb, CONV_WIDTH), BF),
        name="conv_steps")(padded_t, w['conv_w'], w['conv_b'], w['conv_ln_g'], w['conv_ln_b'])


def _memkv_kernel(mem_ref, g_ref, wk_ref, wv_ref, grp_ref, kg_ref, k_out, v_out):
    m = _rms(mem_ref[...], g_ref[...]).astype(BF)
    k = _dot(m, wk_ref[...])
    k_out[...] = k * lax.rsqrt(_dot((k * k).astype(BF), grp_ref[...]) + EPS) * kg_ref[...]
    v_out[...] = _dot(m, wv_ref[...])


def _memkv(mem, w):
    n = mem.shape[0]
    full = lambda i: (0, 0)
    ins = [mem, w['mem_norm_g'], w['w_mem_k'], w['w_mem_v'], w['grp_m'], w['mem_kn_g']]
    o = jax.ShapeDtypeStruct((n, MEM_WIDTH), F32)
    return pl.pallas_call(
        _memkv_kernel, grid=(1,), in_specs=[pl.BlockSpec(a.shape, full) for a in ins],
        out_specs=(pl.BlockSpec((n, MEM_WIDTH), full),) * 2, out_shape=(o, o),
        name="memkv")(*ins)


def _memattn_rows_kernel(q_ref, kx_ref, vx_ref, o_ref, *, n_mem):
    s = _dot(q_ref[...], kx_ref[...])
    parts = []
    for hd in range(MEM_HEADS):
        sh = s[:, hd * n_mem:(hd + 1) * n_mem]
        e = jnp.exp(sh - jnp.max(sh, axis=1, keepdims=True))
        parts.append((e / jnp.sum(e, axis=1, keepdims=True)).astype(BF))
    o_ref[...] = _dot(jnp.concatenate(parts, axis=1), vx_ref[...]).astype(o_ref.dtype)


def _memattn_rows(qm, kx, vx, tm):
    t = qm.shape[0]
    n_mem = vx.shape[0] // MEM_HEADS
    full = lambda i: (0, 0)
    return pl.pallas_call(
        functools.partial(_memattn_rows_kernel, n_mem=n_mem), grid=(t // tm,),
        in_specs=[pl.BlockSpec((tm, MEM_WIDTH), lambda i: (i, 0)), pl.BlockSpec(kx.shape, full),
                  pl.BlockSpec(vx.shape, full)],
        out_specs=pl.BlockSpec((tm, MEM_WIDTH), lambda i: (i, 0)),
        out_shape=jax.ShapeDtypeStruct((t, MEM_WIDTH), BF),
        compiler_params=pltpu.CompilerParams(dimension_semantics=("arbitrary",)),
        name="memattn_rows")(qm, kx, vx)


def _memattn_seqs_kernel(q_ref, k_ref, v_ref, hm_ref, o_ref, *, nseq):
    hm = hm_ref[...]
    for b in range(nseq):
        q8 = q_ref[b].astype(F32)
        qe = jnp.concatenate([q8 * hm[hd:hd + 1, :] for hd in range(MEM_HEADS)], axis=0).astype(BF)
        s = _dot_nt(qe, k_ref[0, b].astype(BF))
        e = jnp.exp(s - jnp.max(s, axis=1, keepdims=True))
        pr = (e / jnp.sum(e, axis=1, keepdims=True)).astype(BF)
        o = _dot(pr, v_ref[0, b].astype(BF))
        acc = o[0:SUBLANES] * hm[0:1, :]
        for hd in range(1, MEM_HEADS):
            acc = acc + o[hd * SUBLANES:(hd + 1) * SUBLANES] * hm[hd:hd + 1, :]
        o_ref[b] = acc.astype(o_ref.dtype)


def _memattn_seqs(q8, cache_k, cache_v, layer, head_mask, nseq):
    nb = q8.shape[0]
    m = cache_k.shape[2]
    return pl.pallas_call(
        functools.partial(_memattn_seqs_kernel, nseq=nseq), grid=(nb // nseq,),
        in_specs=[pl.BlockSpec((nseq, SUBLANES, MEM_WIDTH), lambda i: (i, 0, 0)),
                  pl.BlockSpec((1, nseq, m, MEM_WIDTH), lambda i: (layer, i, 0, 0)),
                  pl.BlockSpec((1, nseq, m, MEM_WIDTH), lambda i: (layer, i, 0, 0)),
                  pl.BlockSpec(head_mask.shape, lambda i: (0, 0))],
        out_specs=pl.BlockSpec((nseq, SUBLANES, MEM_WIDTH), lambda i: (i, 0, 0)),
        out_shape=jax.ShapeDtypeStruct((nb, SUBLANES, MEM_WIDTH), BF),
        compiler_params=pltpu.CompilerParams(dimension_semantics=("arbitrary",)),
        name="memattn_seqs")(q8, cache_k, cache_v, head_mask)


_QCOLS = 32


def _absorb_kernel(q_ref, gk_ref, wt_ref, o_ref):
    for hd in range(N_HEADS):
        qg = (q_ref[hd].astype(F32) * gk_ref[...]).astype(BF)
        o_ref[hd] = _dot(qg, wt_ref[hd]).astype(o_ref.dtype)


def _absorb(q, gk, wukt):
    t = q.shape[1]
    f3 = lambda i: (0, 0, 0)
    return pl.pallas_call(
        _absorb_kernel, grid=(1,),
        in_specs=[pl.BlockSpec(q.shape, f3), pl.BlockSpec(gk.shape, lambda i: (0, 0)), pl.BlockSpec(wukt.shape, f3)],
        out_specs=pl.BlockSpec((N_HEADS, t, KV_RANK), f3),
        out_shape=jax.ShapeDtypeStruct((N_HEADS, t, KV_RANK), BF),
        name="absorb")(q, gk, wukt)


def _paged_kernel(pt_ref, qa_ref, qr_ref, nckv_ref, nkpe_ref, *refs, npg, n_new):
    ckv_pages = refs[:npg]
    kpe_pages = refs[npg:2 * npg]
    wuk_ref, grp_ref, wuv_ref, hm_ref, o_ref, m_scr, l_scr, acc_scr, ckv_scr, kpe_scr = refs[2 * npg:]
    c = pl.program_id(1)
    qa = qa_ref[0]
    qr = qr_ref[0]

    def attend(ckvb, kpeb, causal):
        kn = _dot(ckvb, wuk_ref[...])
        r = lax.rsqrt(_dot((kn * kn).astype(BF), grp_ref[...]) + EPS)
        s = _dot(ckvb, qa) * r + _dot(kpeb, qr)
        if causal:
            row = lax.broadcasted_iota(jnp.int32, s.shape, 0)
            col = lax.broadcasted_iota(jnp.int32, s.shape, 1)
            s = jnp.where(row <= col // N_HEADS, s, NEG)
        st = s.T[:_QCOLS]
        m_prev = m_scr[...]
        m_new = jnp.maximum(m_prev, jnp.max(st, axis=1, keepdims=True))
        alpha = jnp.exp(m_prev - m_new)
        pr = jnp.exp(st - m_new[:, :1])
        l_scr[...] = alpha * l_scr[...] + jnp.sum(pr, axis=1, keepdims=True)
        acc_scr[...] = alpha[:, :1] * acc_scr[...] + _dot(pr.astype(BF), ckvb)
        m_scr[...] = m_new

    @pl.when(c == 0)
    def _():
        m_scr[...] = jnp.full(m_scr.shape, NEG, F32)
        l_scr[...] = jnp.zeros(l_scr.shape, F32)
        acc_scr[...] = jnp.zeros(acc_scr.shape, F32)
        ckv_scr[0:PAGE_SIZE, :] = jnp.zeros((PAGE_SIZE, KV_RANK), BF)
        kpe_scr[0:PAGE_SIZE, :] = jnp.zeros((PAGE_SIZE, QK_ROPE), BF)
        ckv_scr[0:2 * SUBLANES, :] = nckv_ref[0].astype(BF)
        kpe_scr[0:2 * SUBLANES, :] = nkpe_ref[0].astype(BF)
        attend(ckv_scr[0:PAGE_SIZE, :], kpe_scr[0:PAGE_SIZE, :], True)

    for r in range(npg):
        ckv_scr[r * PAGE_SIZE:(r + 1) * PAGE_SIZE, :] = ckv_pages[r][0, 0].astype(BF)
        kpe_scr[r * PAGE_SIZE:(r + 1) * PAGE_SIZE, :] = kpe_pages[r][0, 0].astype(BF)
    attend(ckv_scr[...], kpe_scr[...], False)

    @pl.when(c == pl.num_programs(1) - 1)
    def _():
        lat = (acc_scr[...] / l_scr[...][:, :1]).astype(BF)
        full = _dot(lat, wuv_ref[...])
        hm = hm_ref[...]
        rows = [jnp.sum(full[qi * N_HEADS:(qi + 1) * N_HEADS] * hm, axis=0, keepdims=True)
                for qi in range(n_new)]
        o_ref[0] = jnp.concatenate(rows, axis=0).astype(o_ref.dtype)


def _paged_attention(page_table, qa, qr, nckv, nkpe, cache_ckv, cache_kpe, layer, w, npg, n_new):
    nb, n_pages = page_table.shape
    nchunk = n_pages // npg
    tc = npg * PAGE_SIZE

    def page_spec(r, width):
        return pl.BlockSpec((1, 1, PAGE_SIZE, width), lambda b, c, pt: (layer, pt[b, c * npg + r], 0, 0))

    per_seq = lambda shp: pl.BlockSpec((1,) + shp, lambda b, c, pt: (b, 0, 0))
    const = lambda a: pl.BlockSpec(a.shape, lambda b, c, pt: (0, 0))
    consts = [w['wuk_lat'], w['grp_lat'], w['wuv_lat'], w['hm_lat']]
    in_specs = ([per_seq((KV_RANK, LANES)), per_seq((QK_ROPE, LANES)), per_seq((2 * SUBLANES, KV_RANK)),
                 per_seq((2 * SUBLANES, QK_ROPE))] +
                [page_spec(r, KV_RANK) for r in range(npg)] + [page_spec(r, QK_ROPE) for r in range(npg)] +
                [const(a) for a in consts])
    grid_spec = pltpu.PrefetchScalarGridSpec(
        num_scalar_prefetch=1, grid=(nb, nchunk), in_specs=in_specs,
        out_specs=pl.BlockSpec((1, n_new, MLA_WIDTH), lambda b, c, pt: (b, 0, 0)),
        scratch_shapes=[pltpu.VMEM((_QCOLS, LANES), F32), pltpu.VMEM((_QCOLS, LANES), F32),
                        pltpu.VMEM((_QCOLS, KV_RANK), F32),
                        pltpu.VMEM((tc, KV_RANK), BF), pltpu.VMEM((tc, QK_ROPE), BF)])
    est = 2 * npg * PAGE_SIZE * (KV_RANK + LANES) * 4 + tc * (KV_RANK + LANES) * 2 + tc * (512 + 4 * LANES) * 4
    return pl.pallas_call(
        functools.partial(_paged_kernel, npg=npg, n_new=n_new), grid_spec=grid_spec,
        out_shape=jax.ShapeDtypeStruct((nb, n_new, MLA_WIDTH), BF),
        compiler_params=pltpu.CompilerParams(dimension_semantics=("arbitrary", "arbitrary"),
                                             vmem_limit_bytes=_vmem_limit(est)),
        name="paged")(page_table, qa, qr, nckv, nkpe, *([cache_ckv] * npg), *([cache_kpe] * npg), *consts)


def _route(logits):
    lane_i = lax.broadcasted_iota(jnp.int32, logits.shape, 1)
    lane = lane_i.astype(F32)
    big = jnp.float32(1 << 20)
    is_g = (lane_i >= N_EXPERTS) & (lane_i < N_EXPERTS + N_GROUPS)
    gl = jnp.where(is_g, logits, NEG)
    gmax = jnp.max(gl, axis=1, keepdims=True)
    gsum = jnp.sum(jnp.where(is_g, jnp.exp(gl - gmax), 0.0), axis=1, keepdims=True)
    p_grp = 1.0 / gsum
    grp = jnp.min(jnp.where(is_g & (gl == gmax), lane, big), axis=1, keepdims=True) - N_EXPERTS
    sel = (lane_i < N_EXPERTS) & ((lane_i // EXPERTS_PER_GROUP).astype(F32) == grp)
    el = jnp.where(sel, logits, NEG)
    emax = jnp.max(el, axis=1, keepdims=True)
    ee = jnp.where(sel, jnp.exp(el - emax), 0.0)
    pe = ee / jnp.sum(ee, axis=1, keepdims=True)
    p1 = jnp.max(pe, axis=1, keepdims=True)
    i1 = jnp.min(jnp.where(sel & (pe == p1), lane, big), axis=1, keepdims=True)
    pe2 = jnp.where(sel & (lane != i1), pe, -1.0)
    p2 = jnp.max(pe2, axis=1, keepdims=True)
    i2 = jnp.min(jnp.where(pe2 == p2, lane, big), axis=1, keepdims=True)
    norm = p_grp / (p1 + p2)
    return jnp.where(lane == i1, p1 * norm, jnp.where(lane == i2, p2 * norm, 0.0))


def _outproj_kernel(x_ref, a_ref, c_ref, m_ref, wo_ref, g2_ref, wr_ref, br_ref, x1_out, t_out, gate_out):
    y = _dot(a_ref[...], wo_ref[0:MLA_WIDTH, :])
    y = y + _dot(c_ref[...], wo_ref[MLA_WIDTH:MLA_WIDTH + CONV_WIDTH, :])
    y = y + _dot(m_ref[...], wo_ref[MLA_WIDTH + CONV_WIDTH:, :])
    x1 = x_ref[...] + y
    x1_out[...] = x1
    t = _rms(x1, g2_ref[...])
    t_out[...] = t.astype(BF)
    logits = jnp.dot(t, wr_ref[...], preferred_element_type=F32, precision=lax.Precision.HIGHEST) + br_ref[...]
    gate_out[...] = _route(logits)


def _outproj(x, a_mla, a_conv, a_mem, w, tm):
    t = x.shape[0]
    row = lambda i: (i, 0)
    full = lambda i: (0, 0)
    ins = [x, a_mla, a_conv, a_mem, w['w_out'], w['norm2_g'], w['w_router'], w['b_router']]
    in_specs = [pl.BlockSpec((tm, D_MODEL), row), pl.BlockSpec((tm, MLA_WIDTH), row),
                pl.BlockSpec((tm, CONV_WIDTH), row), pl.BlockSpec((tm, MEM_WIDTH), row)] + \
               [pl.BlockSpec(a.shape, full) for a in ins[4:]]
    out_shape = (jax.ShapeDtypeStruct((t, D_MODEL), F32), jax.ShapeDtypeStruct((t, D_MODEL), BF),
                 jax.ShapeDtypeStruct((t, LANES), F32))
    out_specs = (pl.BlockSpec((tm, D_MODEL), row), pl.BlockSpec((tm, D_MODEL), row), pl.BlockSpec((tm, LANES), row))
    est = 2 * tm * D_MODEL * (4 + 4 + 2 + 2) + 2 * 2 * D_MODEL * D_MODEL + 4 * tm * D_MODEL * 4
    return pl.pallas_call(
        _outproj_kernel, grid=(t // tm,), in_specs=in_specs, out_specs=out_specs, out_shape=out_shape,
        compiler_params=pltpu.CompilerParams(dimension_semantics=("arbitrary",),
                                             vmem_limit_bytes=_vmem_limit(est)),
        name="outproj")(*ins)


def _moe_kernel(t_ref, gate_ref, x1_ref, wg_ref, wu_ref, wd_ref, o_ref):
    e = pl.program_id(1)

    @pl.when(e == 0)
    def _():
        o_ref[...] = x1_ref[...]

    gate = gate_ref[...]
    lane = lax.broadcasted_iota(jnp.int32, gate.shape, 1)
    g = jnp.sum(jnp.where(lane == e, gate, 0.0), axis=1, keepdims=True)
    t = t_ref[...]
    hg = _dot(t, wg_ref[0])
    hu = _dot(t, wu_ref[0])
    h = (hg * _sigmoid(hg) * hu * g).astype(BF)
    o_ref[...] += _dot(h, wd_ref[0])


def _moe(t, gate, x1, w, tm):
    n = t.shape[0]
    row = lambda i, e: (i, 0)
    ex = lambda i, e: (e, 0, 0)
    est = 2 * tm * D_MODEL * (2 + 4 + 4) + 2 * 3 * 2 * D_MODEL * D_EXPERT + 6 * tm * D_EXPERT * 4
    return pl.pallas_call(
        _moe_kernel, grid=(n // tm, N_EXPERTS),
        in_specs=[pl.BlockSpec((tm, D_MODEL), row), pl.BlockSpec((tm, LANES), row), pl.BlockSpec((tm, D_MODEL), row),
                  pl.BlockSpec((1, D_MODEL, D_EXPERT), ex), pl.BlockSpec((1, D_MODEL, D_EXPERT), ex),
                  pl.BlockSpec((1, D_EXPERT, D_MODEL), ex)],
        out_specs=pl.BlockSpec((tm, D_MODEL), row),
        out_shape=jax.ShapeDtypeStruct((n, D_MODEL), F32),
        compiler_params=pltpu.CompilerParams(dimension_semantics=("arbitrary", "arbitrary"),
                                             vmem_limit_bytes=_vmem_limit(est)),
        name="moe")(t, gate, x1, w['w_gate'], w['w_up'], w['w_down'])


def _group_mean_matrix(width, groups):
    m = np.zeros((width, width), np.float32)
    for a, b in groups:
        m[a:b, a:b] = 1.0 / (b - a)
    return jnp.asarray(m, BF)


def _pad_heads(wm, n_heads, width, offset=0, stride=None):
    stride = stride or width
    k = wm.shape[0]
    wm = wm.reshape(k, n_heads, stride)[:, :, offset:offset + width]
    return jnp.pad(wm, ((0, 0), (0, 0), (0, HEAD_PAD - width))).reshape(k, n_heads * HEAD_PAD)


def _lane_row(parts):
    return jnp.concatenate([jnp.asarray(p, F32).reshape(-1) for p in parts])[None, :]


def _layer_weights(l, a):
    w = {}
    w_in = a['w_in'][l]
    cuts = np.cumsum([0, Q_RANK, KV_RANK, QK_ROPE, CONV_WIDTH, CONV_WIDTH, MEM_WIDTH])
    c_q, c_kv, k_r, u_v, u_g, q_m = [w_in[:, cuts[i]:cuts[i + 1]] for i in range(6)]
    k_r = jnp.pad(k_r, ((0, 0), (_ROPE0, LANES - _ROPE0 - QK_ROPE)))
    w['win'] = jnp.concatenate([c_q, c_kv, u_v, u_g, q_m, k_r], axis=1).astype(BF)
    w['n1g'] = a['norm1_g'][l][None, :]
    w['qag'] = a['q_a_norm_g'][l][None, :]
    w['kvg'] = a['kv_a_norm_g'][l][None, :]
    w['wuq'] = _pad_heads(a['w_uq'][l], N_HEADS, QK_NOPE + QK_ROPE).astype(BF)
    w_ukv = a['w_ukv'][l]
    w['wuk'] = _pad_heads(w_ukv, N_HEADS, QK_NOPE, 0, QK_NOPE + V_HEAD).astype(BF)
    w['wuv'] = _pad_heads(w_ukv, N_HEADS, V_HEAD, QK_NOPE, QK_NOPE + V_HEAD).astype(BF)
    zeros = lambda n: np.zeros((n,), np.float32)
    w['gq'] = _lane_row([a['qn_nope_g'][l], a['qn_rope_g'][l], zeros(LANES - 96)]) * MLA_SCALE
    w['gk'] = _lane_row([a['kn_nope_g'][l], zeros(LANES - QK_NOPE)])
    w['gkr'] = _lane_row([zeros(_ROPE0), a['kn_rope_g'][l], zeros(LANES - 96)])
    qg, kg = [], []
    for hd in range(N_HEADS):
        qg += [(hd * HEAD_PAD, hd * HEAD_PAD + QK_NOPE), (hd * HEAD_PAD + QK_NOPE, hd * HEAD_PAD + 96)]
        kg += [(hd * HEAD_PAD, hd * HEAD_PAD + QK_NOPE)]
    w['grp_q'] = _group_mean_matrix(N_HEADS * HEAD_PAD, qg)
    w['grp_k'] = _group_mean_matrix(N_HEADS * HEAD_PAD, kg)
    w['grp_m'] = _group_mean_matrix(MEM_WIDTH, [(i * MEM_HEAD_DIM, (i + 1) * MEM_HEAD_DIM) for i in range(MEM_HEADS)])
    w['mqg'] = jnp.tile(a['mem_qn_g'][l], MEM_HEADS)[None, :] * MEM_SCALE
    w['mem_kn_g'] = jnp.tile(a['mem_kn_g'][l], MEM_HEADS)[None, :]
    w['mem_norm_g'] = a['mem_norm_g'][l][None, :]
    w['w_mem_k'] = a['w_mem_k'][l].astype(BF)
    w['w_mem_v'] = a['w_mem_v'][l].astype(BF)
    w['conv_w'] = a['conv_w'][l]
    w['conv_b'] = a['conv_b'][l][None, :]
    w['conv_ln_g'] = a['conv_ln_g'][l][None, :]
    w['conv_ln_b'] = a['conv_ln_b'][l][None, :]
    w['w_out'] = a['w_out'][l].astype(BF)
    w['norm2_g'] = a['norm2_g'][l][None, :]
    w['w_router'] = jnp.pad(jnp.concatenate([a['w_router_expert'][l], a['w_router_group'][l]], axis=1),
                            ((0, 0), (0, LANES - N_EXPERTS - N_GROUPS)))
    w['b_router'] = jnp.pad(jnp.concatenate([a['b_router_expert'][l], a['b_router_group'][l]]),
                            (0, LANES - N_EXPERTS - N_GROUPS))[None, :]
    w['w_gate'] = a['w_gate'][l].astype(BF)
    w['w_up'] = a['w_up'][l].astype(BF)
    w['w_down'] = a['w_down'][l].astype(BF)
    w_uk = w_ukv.reshape(KV_RANK, N_HEADS, QK_NOPE + V_HEAD)[:, :, :QK_NOPE]
    w['wuk_lat'] = w_uk.reshape(KV_RANK, N_HEADS * QK_NOPE).astype(BF)
    w['wuv_lat'] = w_ukv.reshape(KV_RANK, N_HEADS, QK_NOPE + V_HEAD)[:, :, QK_NOPE:].reshape(
        KV_RANK, MLA_WIDTH).astype(BF)
    w['wukt'] = jnp.pad(jnp.transpose(w_uk, (1, 2, 0)), ((0, 0), (0, HEAD_PAD - QK_NOPE), (0, 0))).astype(BF)
    head_of_row = np.arange(N_HEADS * QK_NOPE) // QK_NOPE
    head_of_col = np.arange(LANES) % N_HEADS
    w['grp_lat'] = jnp.asarray((head_of_row[:, None] == head_of_col[None, :]) / QK_NOPE, BF)
    w['hm_lat'] = jnp.asarray(np.arange(N_HEADS)[:, None] == head_of_row[None, :], F32)
    return w


def _rope_lane_tables(pos):
    inv_freq = 1.0 / (ROPE_THETA ** (jnp.arange(0, QK_ROPE, 2, dtype=F32) / QK_ROPE))
    ang = pos.astype(F32)[:, None] * inv_freq[None, :]
    cos, sin = jnp.cos(ang), jnp.sin(ang)
    n = pos.shape[0]
    one = jnp.ones((n, _ROPE0), F32)
    zero_hi = jnp.zeros((n, LANES - _ROPE0 - QK_ROPE), F32)
    c = jnp.concatenate([one, cos, cos, zero_hi], axis=1)
    s = jnp.concatenate([jnp.zeros((n, _ROPE0), F32), -sin, sin, zero_hi], axis=1)
    return c, s


def _pick_tile(n, pref):
    return pref if n % pref == 0 else n


def _block_diag_memory(k, v):
    m = k.shape[0]
    hm = (np.arange(MEM_WIDTH)[:, None] // MEM_HEAD_DIM) == (np.arange(MEM_HEADS * m)[None, :] // m)
    kx = jnp.where(hm, jnp.tile(k.T, (1, MEM_HEADS)), 0.0).astype(BF)
    vx = jnp.where(hm.T, jnp.tile(v, (MEM_HEADS, 1)), 0.0).astype(BF)
    return kx, vx


def kernel(x_prompt, x_sample, cache_ckv, cache_kpe, state_conv, cache_mem_k, cache_mem_v, page_table, mem_prompt, norm1_g, w_in, q_a_norm_g, w_uq, kv_a_norm_g, w_ukv, qn_nope_g, qn_rope_g, kn_nope_g, kn_rope_g, conv_w, conv_b, conv_ln_g, conv_ln_b, mem_norm_g, w_mem_k, w_mem_v, mem_qn_g, mem_kn_g, w_out, norm2_g, w_router_group, b_router_group, w_router_expert, b_router_expert, w_gate, w_up, w_down):
    params = dict(norm1_g=norm1_g, w_in=w_in, q_a_norm_g=q_a_norm_g, w_uq=w_uq, kv_a_norm_g=kv_a_norm_g,
                  w_ukv=w_ukv, qn_nope_g=qn_nope_g, qn_rope_g=qn_rope_g, kn_nope_g=kn_nope_g, kn_rope_g=kn_rope_g,
                  conv_w=conv_w, conv_b=conv_b, conv_ln_g=conv_ln_g, conv_ln_b=conv_ln_b, mem_norm_g=mem_norm_g,
                  w_mem_k=w_mem_k, w_mem_v=w_mem_v, mem_qn_g=mem_qn_g, mem_kn_g=mem_kn_g, w_out=w_out,
                  norm2_g=norm2_g, w_router_group=w_router_group, b_router_group=b_router_group,
                  w_router_expert=w_router_expert, b_router_expert=b_router_expert, w_gate=w_gate, w_up=w_up,
                  w_down=w_down)
    depth = w_in.shape[0]
    n_b, n_seq, _ = x_prompt.shape
    assert n_b == 1, "the prompt group is a single sequence"
    n_db, n_dec, _ = x_sample.shape
    n_pages = page_table.shape[1]
    past_len = n_pages * PAGE_SIZE
    n_mem = mem_prompt.shape[1]
    tp = n_b * n_seq
    ts = n_db * n_dec
    assert n_dec <= SUBLANES and n_dec * N_HEADS == _QCOLS

    cos_p, sin_p = _rope_lane_tables(jnp.arange(n_seq))
    cos_s, sin_s = _rope_lane_tables(jnp.tile(past_len + jnp.arange(n_dec), n_db))
    mem_head_mask = jnp.asarray(np.arange(MEM_HEADS)[:, None] == (np.arange(MEM_WIDTH)[None, :] // MEM_HEAD_DIM), F32)
    cache_mk = cache_mem_k.reshape(cache_mem_k.shape[:3] + (MEM_WIDTH,))
    cache_mv = cache_mem_v.reshape(cache_mem_v.shape[:3] + (MEM_WIDTH,))

    tm_p = _pick_tile(tp, 512)
    tq = _pick_tile(tp, 1024)
    tm_moe = _pick_tile(tp, 1024)
    npg = 16 if n_pages % 16 == 0 else n_pages
    nseq_mem = 8 if n_db % 8 == 0 else n_db

    xp = x_prompt.reshape(tp, D_MODEL)
    xs = x_sample.reshape(ts, D_MODEL)
    outs = {k: [] for k in ('ckv_p', 'kpe_p', 'conv_p', 'memk_p', 'memv_p', 'ckv_s', 'kpe_s', 'conv_s')}
    for l in range(depth):
        w = _layer_weights(l, params)

        mk, mv = _memkv(mem_prompt.reshape(n_mem, D_MODEL), w)
        q, k, v, ckv, kpe, glu, qm = _premix(xp, w, cos_p, sin_p, tm_p)
        a_mla = _flash_attention(q, k, v, tq)
        padded = jnp.concatenate([jnp.zeros((_CONV_HALO, CONV_WIDTH), F32), glu], axis=0)
        a_conv = _conv_rows(padded, w, tm_p)
        kx, vx = _block_diag_memory(mk, mv)
        a_mem = _memattn_rows(qm, kx, vx, tm_p)
        x1, t, gate = _outproj(xp, a_mla, a_conv, a_mem, w, tm_p)
        xp = _moe(t, gate, x1, w, tm_moe)
        outs['ckv_p'].append(ckv.reshape(n_b, n_seq, KV_RANK))
        outs['kpe_p'].append(kpe.reshape(n_b, n_seq, QK_ROPE))
        outs['conv_p'].append(padded[-(CONV_K - 1):].reshape(n_b, CONV_K - 1, CONV_WIDTH))
        outs['memk_p'].append(mk.reshape(n_b, n_mem, MEM_HEADS, MEM_HEAD_DIM))
        outs['memv_p'].append(mv.reshape(n_b, n_mem, MEM_HEADS, MEM_HEAD_DIM))

        q, k, v, ckv, kpe, glu, qm = _premix(xs, w, cos_s, sin_s, _pick_tile(ts, 512))
        qlat = _absorb(q, w['gk'], w['wukt'])
        qa = jnp.transpose(qlat.reshape(N_HEADS, n_db, n_dec, KV_RANK), (1, 3, 2, 0)).reshape(n_db, KV_RANK, _QCOLS)
        qr = jnp.transpose(q[:, :, _ROPE0:_ROPE0 + QK_ROPE].reshape(N_HEADS, n_db, n_dec, QK_ROPE),
                           (1, 3, 2, 0)).reshape(n_db, QK_ROPE, _QCOLS)
        lane_pad = ((0, 0), (0, 0), (0, LANES - _QCOLS))
        row_pad = ((0, 0), (0, 2 * SUBLANES - n_dec), (0, 0))
        a_mla = _paged_attention(page_table, jnp.pad(qa, lane_pad), jnp.pad(qr, lane_pad),
                                 jnp.pad(ckv.reshape(n_db, n_dec, KV_RANK), row_pad),
                                 jnp.pad(kpe.reshape(n_db, n_dec, QK_ROPE), row_pad),
                                 cache_ckv, cache_kpe, l, w, npg, n_dec).reshape(ts, MLA_WIDTH)
        padded = jnp.concatenate([state_conv[l], glu.reshape(n_db, n_dec, CONV_WIDTH)], axis=1)
        a_conv = jnp.transpose(_conv_steps(jnp.transpose(padded, (1, 0, 2)), w), (1, 0, 2)).reshape(ts, CONV_WIDTH)
        q8 = jnp.pad(qm.reshape(n_db, n_dec, MEM_WIDTH), ((0, 0), (0, SUBLANES - n_dec), (0, 0)))
        a_mem = _memattn_seqs(q8, cache_mk, cache_mv, l, mem_head_mask, nseq_mem)[:, :n_dec].reshape(ts, MEM_WIDTH)
        x1, t, gate = _outproj(xs, a_mla, a_conv, a_mem, w, _pick_tile(ts, 512))
        xs = _moe(t, gate, x1, w, _pick_tile(ts, 512))
        outs['ckv_s'].append(ckv.reshape(n_db, n_dec, KV_RANK))
        outs['kpe_s'].append(kpe.reshape(n_db, n_dec, QK_ROPE))
        outs['conv_s'].append(padded[:, -(CONV_K - 1):])

    st = lambda key: jnp.stack(outs[key])
    return (xp.reshape(n_b, n_seq, D_MODEL), xs.reshape(n_db, n_dec, D_MODEL), st('ckv_p'), st('kpe_p'),
            st('conv_p'), st('memk_p'), st('memv_p'), st('ckv_s'), st('kpe_s'), st('conv_s'))
```

```python
import functools

import numpy as np
import jax
import jax.numpy as jnp
from jax import lax
from jax.experimental import pallas as pl
from jax.experimental.pallas import tpu as pltpu

D_MODEL = 1024
N_HEADS = 8
QK_NOPE = 64
QK_ROPE = 32
V_HEAD = 64
Q_RANK = 384
KV_RANK = 256
MLA_WIDTH = N_HEADS * V_HEAD
CONV_WIDTH = 256
MEM_WIDTH = 256
CONV_K = 31
MEM_HEADS = 4
MEM_HEAD_DIM = 64
N_GROUPS = 4
EXPERTS_PER_GROUP = 8
N_EXPERTS = 32
D_EXPERT = 256
PAGE_SIZE = 128
ROPE_THETA = 10000.0
EPS = 1e-6
MLA_SCALE = (QK_NOPE + QK_ROPE) ** -0.5
MEM_SCALE = MEM_HEAD_DIM ** -0.5

LANES = 128
SUBLANES = 8
VMEM_BYTES = 64 << 20
HEAD_PAD = LANES

_CQ0, _CKV0, _UV0, _UG0, _QM0, _KR0, _IN_PAD = 0, 384, 640, 896, 1152, 1408, 1536
_ROPE0 = QK_NOPE
_HALF = QK_ROPE // 2
NEG = -1e30
LOG2E = 1.4426950408889634
V_ROWS = 80

BF = jnp.bfloat16
F32 = jnp.float32


def _vmem_limit(nbytes):
    return int(min(max(2 * nbytes, 16 << 20), VMEM_BYTES - (8 << 20)))


def _dot(a, b):
    return jnp.dot(a, b, preferred_element_type=F32)


def _dot_nt(a, b):
    return lax.dot_general(a, b, (((1,), (1,)), ((), ())), preferred_element_type=F32)


def _rms(x, g):
    return x * lax.rsqrt(jnp.mean(x * x, axis=-1, keepdims=True) + EPS) * g


def _sigmoid(x):
    return 1.0 / (1.0 + jnp.exp(-x))


def _rope_padded(x, c, s):
    lane = lax.broadcasted_iota(jnp.int32, x.shape, 1)
    swapped = jnp.where(lane < _ROPE0 + _HALF, pltpu.roll(x, LANES - _HALF, 1), pltpu.roll(x, _HALF, 1))
    return x * c + swapped * s


def _premix_kernel(x_ref, n1g_ref, win_ref, qag_ref, wuq_ref, kvg_ref, wuk_ref, wuv_ref, gq_ref, gk_ref,
                   gkr_ref, grp_q_ref, grp_k_ref, grp_m_ref, mqg_ref, cos_ref, sin_ref,
                   q_out, k_out, v_out, ckv_out, kpe_out, glu_out, qm_out):
    x = x_ref[...]
    h = _rms(x, n1g_ref[...]).astype(BF)

    def seg(a, b):
        return _dot(h, win_ref[:, a:b])

    c = cos_ref[...]
    s = sin_ref[...]

    cq = _rms(seg(_CQ0, _CKV0), qag_ref[...]).astype(BF)
    q = _dot(cq, wuq_ref[...])
    q = q * lax.rsqrt(_dot((q * q).astype(BF), grp_q_ref[...]) + EPS)
    gq = gq_ref[...]
    for hd in range(N_HEADS):
        qh = q[:, hd * HEAD_PAD:(hd + 1) * HEAD_PAD] * gq
        q_out[hd] = _rope_padded(qh, c, s).T.astype(BF)

    ckv = _rms(seg(_CKV0, _UV0), kvg_ref[...])
    ckv_out[...] = ckv
    ckvb = ckv.astype(BF)
    kr = seg(_KR0, _IN_PAD)
    kr = kr * lax.rsqrt(jnp.sum(kr * kr, axis=-1, keepdims=True) * (1.0 / QK_ROPE) + EPS) * gkr_ref[...]
    kpe = _rope_padded(kr, c, s)
    kpe_out[...] = kpe[:, _ROPE0:_ROPE0 + QK_ROPE]

    kx = _dot(ckvb, wuk_ref[...])
    kx = kx * lax.rsqrt(_dot((kx * kx).astype(BF), grp_k_ref[...]) + EPS)
    vx = _dot(ckvb, wuv_ref[...])
    gk = gk_ref[...]
    ones_row = lax.broadcasted_iota(jnp.int32, (V_ROWS, x.shape[0]), 0) == V_HEAD
    for hd in range(N_HEADS):
        sl = slice(hd * HEAD_PAD, (hd + 1) * HEAD_PAD)
        k_out[hd] = (kx[:, sl] * gk + kpe).astype(BF)
        v_out[hd] = jnp.where(ones_row, 1.0, vx[:, sl].T[:V_ROWS]).astype(BF)

    glu_out[...] = seg(_UV0, _UG0) * _sigmoid(seg(_UG0, _QM0))
    qm = seg(_QM0, _KR0)
    qm = qm * lax.rsqrt(_dot((qm * qm).astype(BF), grp_m_ref[...]) + EPS) * mqg_ref[...]
    qm_out[...] = qm.astype(BF)


def _premix(x, w, cos_t, sin_t, tm):
    t = x.shape[0]
    nt = t // tm
    row = lambda i: (i, 0)
    full = lambda i: (0, 0)
    hm = lambda i: (0, i, 0)
    wspec = lambda a: pl.BlockSpec(a.shape, full)
    ins = [x, w['n1g'], w['win'], w['qag'], w['wuq'], w['kvg'], w['wuk'], w['wuv'], w['gq'], w['gk'], w['gkr'],
           w['grp_q'], w['grp_k'], w['grp_m'], w['mqg'], cos_t, sin_t]
    in_specs = [pl.BlockSpec((tm, D_MODEL), row)] + [wspec(a) for a in ins[1:15]] + \
               [pl.BlockSpec((tm, LANES), row), pl.BlockSpec((tm, LANES), row)]
    fm = lambda i: (0, 0, i)
    out_shape = (jax.ShapeDtypeStruct((N_HEADS, HEAD_PAD, t), BF), jax.ShapeDtypeStruct((N_HEADS, t, HEAD_PAD), BF),
                 jax.ShapeDtypeStruct((N_HEADS, V_ROWS, t), BF),
                 jax.ShapeDtypeStruct((t, KV_RANK), F32), jax.ShapeDtypeStruct((t, QK_ROPE), F32),
                 jax.ShapeDtypeStruct((t, CONV_WIDTH), F32), jax.ShapeDtypeStruct((t, MEM_WIDTH), BF))
    out_specs = (pl.BlockSpec((N_HEADS, HEAD_PAD, tm), fm), pl.BlockSpec((N_HEADS, tm, HEAD_PAD), hm),
                 pl.BlockSpec((N_HEADS, V_ROWS, tm), fm),
                 pl.BlockSpec((tm, KV_RANK), row), pl.BlockSpec((tm, QK_ROPE), row),
                 pl.BlockSpec((tm, CONV_WIDTH), row), pl.BlockSpec((tm, MEM_WIDTH), row))
    wbytes = sum(int(np.prod(a.shape)) * a.dtype.itemsize for a in ins[1:15])
    est = 2 * wbytes + tm * (2 * 4 * D_MODEL + 3 * 2 * 2 * N_HEADS * HEAD_PAD + 8 * 4 * N_HEADS * HEAD_PAD)
    return pl.pallas_call(
        _premix_kernel, grid=(nt,), in_specs=in_specs, out_specs=out_specs, out_shape=out_shape,
        compiler_params=pltpu.CompilerParams(dimension_semantics=("arbitrary",),
                                             vmem_limit_bytes=_vmem_limit(est)),
        name="premix")(*ins)


def _flash_kernel(qi_ref, kj_ref, qt_ref, k_ref, vt_ref, o_ref, m_scr, acc_scr):
    p = pl.program_id(0)
    i = qi_ref[p]
    j = kj_ref[p]

    @pl.when(j == 0)
    def _():
        m_scr[...] = jnp.full(m_scr.shape, NEG, F32)
        acc_scr[...] = jnp.zeros(acc_scr.shape, F32)

    def sweep(masked):
        def body(hd, carry):
            st = _dot(k_ref[hd], qt_ref[hd])
            if masked:
                kk = lax.broadcasted_iota(jnp.int32, st.shape, 0)
                qq = lax.broadcasted_iota(jnp.int32, st.shape, 1)
                st = jnp.where(kk <= qq, st, NEG)
            m_prev = m_scr[hd]
            m_new = jnp.maximum(m_prev, jnp.max(st, axis=0, keepdims=True))
            alpha = jnp.exp2(m_prev - m_new)
            pt = jnp.exp2(st - m_new).astype(BF)
            acc_scr[hd] = alpha * acc_scr[hd] + _dot(vt_ref[hd], pt)
            m_scr[hd] = m_new
            return carry
        lax.fori_loop(0, N_HEADS, body, 0)

    @pl.when(j < i)
    def _():
        sweep(False)

    @pl.when(j == i)
    def _():
        sweep(True)
        for hp in range(N_HEADS // 2):
            halves = []
            for hd in (2 * hp, 2 * hp + 1):
                acc = acc_scr[hd]
                halves.append((acc[:V_HEAD] / acc[V_HEAD:V_HEAD + 1]).T)
            o_ref[:, hp * LANES:(hp + 1) * LANES] = jnp.concatenate(halves, axis=1).astype(o_ref.dtype)


def _flash_attention(qt, k, vt, tq):
    t = k.shape[1]
    nq = t // tq
    qi = np.concatenate([np.full(i + 1, i, np.int32) for i in range(nq)])
    kj = np.concatenate([np.arange(i + 1, dtype=np.int32) for i in range(nq)])
    grid_spec = pltpu.PrefetchScalarGridSpec(
        num_scalar_prefetch=2, grid=(len(qi),),
        in_specs=[pl.BlockSpec((N_HEADS, HEAD_PAD, tq), lambda p, qi, kj: (0, 0, qi[p])),
                  pl.BlockSpec((N_HEADS, tq, HEAD_PAD), lambda p, qi, kj: (0, kj[p], 0)),
                  pl.BlockSpec((N_HEADS, V_ROWS, tq), lambda p, qi, kj: (0, 0, kj[p]))],
        out_specs=pl.BlockSpec((tq, MLA_WIDTH), lambda p, qi, kj: (qi[p], 0)),
        scratch_shapes=[pltpu.VMEM((N_HEADS, 1, tq), F32), pltpu.VMEM((N_HEADS, V_ROWS, tq), F32)])
    nblk = N_HEADS * tq * HEAD_PAD
    est = 3 * 2 * 2 * nblk + 4 * nblk + 2 * 2 * tq * MLA_WIDTH + 3 * 4 * tq * tq
    return pl.pallas_call(
        _flash_kernel, grid_spec=grid_spec,
        out_shape=jax.ShapeDtypeStruct((t, MLA_WIDTH), BF),
        compiler_params=pltpu.CompilerParams(dimension_semantics=("arbitrary",),
                                             vmem_limit_bytes=_vmem_limit(est)),
        name="flash")(jnp.asarray(qi), jnp.asarray(kj), qt, k, vt)


_CONV_HALO = 32
_CONV_LEAD = _CONV_HALO - (CONV_K - 1)


def _ln_swish(y, g, b):
    mu = jnp.mean(y, axis=-1, keepdims=True)
    d = y - mu
    y = d * lax.rsqrt(jnp.mean(d * d, axis=-1, keepdims=True) + EPS) * g + b
    return y * _sigmoid(y)


def _conv_rows_kernel(main_ref, halo_ref, w_ref, b_ref, g_ref, lb_ref, o_ref, buf, *, tm):
    buf[0:tm, :] = main_ref[...]
    buf[tm:tm + _CONV_HALO, :] = halo_ref[...]
    acc = jnp.zeros((tm, CONV_WIDTH), F32)
    for j in range(CONV_K):
        acc = acc + buf[_CONV_LEAD + j:_CONV_LEAD + j + tm, :] * w_ref[j:j + 1, :]
    o_ref[...] = _ln_swish(acc + b_ref[...], g_ref[...], lb_ref[...]).astype(o_ref.dtype)


def _conv_rows(padded, w, tm):
    t = padded.shape[0] - _CONV_HALO
    full = lambda i: (0, 0)
    hb = tm // _CONV_HALO
    return pl.pallas_call(
        functools.partial(_conv_rows_kernel, tm=tm), grid=(t // tm,),
        in_specs=[pl.BlockSpec((tm, CONV_WIDTH), lambda i: (i, 0)),
                  pl.BlockSpec((_CONV_HALO, CONV_WIDTH), lambda i: ((i + 1) * hb, 0)),
                  pl.BlockSpec((CONV_K, CONV_WIDTH), full)] + [pl.BlockSpec((1, CONV_WIDTH), full)] * 3,
        out_specs=pl.BlockSpec((tm, CONV_WIDTH), lambda i: (i, 0)),
        out_shape=jax.ShapeDtypeStruct((t, CONV_WIDTH), BF),
        scratch_shapes=[pltpu.VMEM((tm + _CONV_HALO, CONV_WIDTH), F32)],
        compiler_params=pltpu.CompilerParams(dimension_semantics=("arbitrary",)),
        name="conv_rows")(padded, padded, w['conv_w'], w['conv_b'], w['conv_ln_g'], w['conv_ln_b'])


def _conv_steps_kernel(p_ref, w_ref, b_ref, g_ref, lb_ref, o_ref, *, n_new):
    for t in range(n_new):
        acc = jnp.zeros(o_ref.shape[1:], F32)
        for j in range(CONV_K):
            acc = acc + p_ref[t + j] * w_ref[j:j + 1, :]
        o_ref[t] = _ln_swish(acc + b_ref[...], g_ref[...], lb_ref[...]).astype(o_ref.dtype)


def _conv_steps(padded_t, w):
    n_new = padded_t.shape[0] - (CONV_K - 1)
    nb = padded_t.shape[1]
    full2 = lambda i: (0, 0)
    full3 = lambda i: (0, 0, 0)
    return pl.pallas_call(
        functools.partial(_conv_steps_kernel, n_new=n_new), grid=(1,),
        in_specs=[pl.BlockSpec(padded_t.shape, full3), pl.BlockSpec((CONV_K, CONV_WIDTH), full2)] +
                 [pl.BlockSpec((1, CONV_WIDTH), full2)] * 3,
        out_specs=pl.BlockSpec((n_new, nb, CONV_WIDTH), full3),
        out_shape=jax.ShapeDtypeStruct((n_new, nb, CONV_WIDTH), BF),
        name="conv_steps")(padded_t, w['conv_w'], w['conv_b'], w['conv_ln_g'], w['conv_ln_b'])


def _memkv_kernel(mem_ref, g_ref, wk_ref, wv_ref, grp_ref, kg_ref, k_out, v_out):
    m = _rms(mem_ref[...], g_ref[...]).astype(BF)
    k = _dot(m, wk_ref[...])
    k_out[...] = k * lax.rsqrt(_dot((k * k).astype(BF), grp_ref[...]) + EPS) * kg_ref[...]
    v_out[...] = _dot(m, wv_ref[...])


def _memkv(mem, w):
    n = mem.shape[0]
    full = lambda i: (0, 0)
    ins = [mem, w['mem_norm_g'], w['w_mem_k'], w['w_mem_v'], w['grp_m'], w['mem_kn_g']]
    o = jax.ShapeDtypeStruct((n, MEM_WIDTH), F32)
    return pl.pallas_call(
        _memkv_kernel, grid=(1,), in_specs=[pl.BlockSpec(a.shape, full) for a in ins],
        out_specs=(pl.BlockSpec((n, MEM_WIDTH), full),) * 2, out_shape=(o, o),
        name="memkv")(*ins)


def _memattn_rows_kernel(q_ref, kx_ref, vx_ref, o_ref, *, n_mem):
    s = _dot(q_ref[...], kx_ref[...])
    parts = []
    for hd in range(MEM_HEADS):
        sh = s[:, hd * n_mem:(hd + 1) * n_mem]
        e = jnp.exp(sh - jnp.max(sh, axis=1, keepdims=True))
        parts.append((e / jnp.sum(e, axis=1, keepdims=True)).astype(BF))
    o_ref[...] = _dot(jnp.concatenate(parts, axis=1), vx_ref[...]).astype(o_ref.dtype)


def _memattn_rows(qm, kx, vx, tm):
    t = qm.shape[0]
    n_mem = vx.shape[0] // MEM_HEADS
    full = lambda i: (0, 0)
    return pl.pallas_call(
        functools.partial(_memattn_rows_kernel, n_mem=n_mem), grid=(t // tm,),
        in_specs=[pl.BlockSpec((tm, MEM_WIDTH), lambda i: (i, 0)), pl.BlockSpec(kx.shape, full),
                  pl.BlockSpec(vx.shape, full)],
        out_specs=pl.BlockSpec((tm, MEM_WIDTH), lambda i: (i, 0)),
        out_shape=jax.ShapeDtypeStruct((t, MEM_WIDTH), BF),
        compiler_params=pltpu.CompilerParams(dimension_semantics=("arbitrary",)),
        name="memattn_rows")(qm, kx, vx)


def _memattn_seqs_kernel(q_ref, k_ref, v_ref, hm_ref, o_ref, *, nseq):
    hm = hm_ref[...]
    for b in range(nseq):
        q8 = q_ref[b].astype(F32)
        qe = jnp.concatenate([q8 * hm[hd:hd + 1, :] for hd in range(MEM_HEADS)], axis=0).astype(BF)
        s = _dot(qe, k_ref[0, b].astype(BF))
        e = jnp.exp(s - jnp.max(s, axis=1, keepdims=True))
        pr = (e / jnp.sum(e, axis=1, keepdims=True)).astype(BF)
        o = _dot_nt(pr, v_ref[0, b].astype(BF))
        acc = o[0:SUBLANES] * hm[0:1, :]
        for hd in range(1, MEM_HEADS):
            acc = acc + o[hd * SUBLANES:(hd + 1) * SUBLANES] * hm[hd:hd + 1, :]
        o_ref[b] = acc.astype(o_ref.dtype)


def _memattn_seqs(q8, cache_k, cache_v, layer, head_mask, nseq):
    nb = q8.shape[0]
    m = cache_k.shape[3]
    return pl.pallas_call(
        functools.partial(_memattn_seqs_kernel, nseq=nseq), grid=(nb // nseq,),
        in_specs=[pl.BlockSpec((nseq, SUBLANES, MEM_WIDTH), lambda i: (i, 0, 0)),
                  pl.BlockSpec((1, nseq, MEM_WIDTH, m), lambda i: (layer, i, 0, 0)),
                  pl.BlockSpec((1, nseq, MEM_WIDTH, m), lambda i: (layer, i, 0, 0)),
                  pl.BlockSpec(head_mask.shape, lambda i: (0, 0))],
        out_specs=pl.BlockSpec((nseq, SUBLANES, MEM_WIDTH), lambda i: (i, 0, 0)),
        out_shape=jax.ShapeDtypeStruct((nb, SUBLANES, MEM_WIDTH), BF),
        compiler_params=pltpu.CompilerParams(dimension_semantics=("arbitrary",)),
        name="memattn_seqs")(q8, cache_k, cache_v, head_mask)


_QCOLS = 32
_PAGES_PER_STEP = 32


def _absorb_kernel(qt_ref, gk_ref, wuk_ref, o_ref):
    for hd in range(N_HEADS):
        wg = (wuk_ref[hd] * gk_ref[...]).astype(BF)
        o_ref[hd] = _dot(wg, qt_ref[hd]).astype(o_ref.dtype)


def _absorb(qt, gk, wuk):
    t = qt.shape[2]
    f3 = lambda i: (0, 0, 0)
    return pl.pallas_call(
        _absorb_kernel, grid=(1,),
        in_specs=[pl.BlockSpec(qt.shape, f3), pl.BlockSpec(gk.shape, lambda i: (0, 0)), pl.BlockSpec(wuk.shape, f3)],
        out_specs=pl.BlockSpec((N_HEADS, KV_RANK, t), f3),
        out_shape=jax.ShapeDtypeStruct((N_HEADS, KV_RANK, t), BF),
        name="absorb")(qt, gk, wuk)


def _paged_kernel(pt_ref, qa_ref, qr_ref, nckv_ref, nkpe_ref, *refs, npg, n_new):
    ckv_pages = refs[:npg]
    kpe_pages = refs[npg:2 * npg]
    wuk_ref, grp_ref, wuv_ref, hm_ref, o_ref, m_scr, l_scr, acc_scr, ckv_scr, kpe_scr = refs[2 * npg:]
    c = pl.program_id(1)
    qa = qa_ref[0]
    qr = qr_ref[0]

    def key_energy(ckvb):
        kn = _dot(ckvb, wuk_ref[...])
        k2 = kn * kn
        return ((k2[:, 0:LANES] + k2[:, LANES:2 * LANES]) + (k2[:, 2 * LANES:3 * LANES] + k2[:, 3 * LANES:])).astype(BF)

    def attend(ckvb, kpet, causal):
        r = lax.rsqrt(_dot_nt(grp_ref[...], key_energy(ckvb)) + EPS)
        st = _dot_nt(qa, ckvb) * r + _dot(qr, kpet)
        if causal:
            row = lax.broadcasted_iota(jnp.int32, st.shape, 0)
            col = lax.broadcasted_iota(jnp.int32, st.shape, 1)
            st = jnp.where(col <= row // N_HEADS, st, NEG)
        m = jnp.max(st, axis=1, keepdims=True)
        pr = jnp.exp2(st - m)
        return m, jnp.sum(pr, axis=1, keepdims=True), _dot(pr.astype(BF), ckvb)

    def merge(parts):
        m_all = functools.reduce(jnp.maximum, [p[0] for p in parts])
        scales = [jnp.exp2(p[0] - m_all) for p in parts]
        m_scr[...] = m_all
        l_scr[...] = sum(s * p[1] for s, p in zip(scales, parts))
        acc_scr[...] = sum(s * p[2] for s, p in zip(scales, parts))

    @pl.when(c == 0)
    def _():
        ckv_scr[0:PAGE_SIZE, :] = jnp.zeros((PAGE_SIZE, KV_RANK), BF)
        ckv_scr[0:2 * SUBLANES, :] = nckv_ref[0].astype(BF)
        merge([attend(ckv_scr[0:PAGE_SIZE, :], nkpe_ref[0].astype(BF), True)])

    for r in range(npg):
        ckv_scr[r * PAGE_SIZE:(r + 1) * PAGE_SIZE, :] = ckv_pages[r][0, 0].astype(BF)
        kpe_scr[:, r * PAGE_SIZE:(r + 1) * PAGE_SIZE] = kpe_pages[r][0, 0].astype(BF)
    merge([(m_scr[...], l_scr[...], acc_scr[...]), attend(ckv_scr[...], kpe_scr[...], False)])

    @pl.when(c == pl.num_programs(1) - 1)
    def _():
        lat = (acc_scr[...] / l_scr[...]).astype(BF)
        full = _dot(lat, wuv_ref[...])
        hm = hm_ref[...]
        rows = [jnp.sum(full[qi * N_HEADS:(qi + 1) * N_HEADS] * hm, axis=0, keepdims=True)
                for qi in range(n_new)]
        o_ref[0] = jnp.concatenate(rows, axis=0).astype(o_ref.dtype)


def _paged_attention(page_table, qa, qr, nckv, nkpe, cache_ckv, cache_kpe, layer, w, npg, n_new):
    nb, n_pages = page_table.shape
    nchunk = n_pages // npg
    tc = npg * PAGE_SIZE

    def page_spec(r, shape):
        return pl.BlockSpec((1, 1) + shape, lambda b, c, pt: (layer, pt[b, c * npg + r], 0, 0))

    per_seq = lambda shp: pl.BlockSpec((1,) + shp, lambda b, c, pt: (b, 0, 0))
    const = lambda a: pl.BlockSpec(a.shape, lambda b, c, pt: (0, 0))
    consts = [w['wuk_lat'], w['grp_lat'], w['wuv_lat'], w['hm_lat']]
    in_specs = ([per_seq((_QCOLS, KV_RANK)), per_seq((_QCOLS, QK_ROPE)), per_seq((2 * SUBLANES, KV_RANK)),
                 per_seq((QK_ROPE, PAGE_SIZE))] +
                [page_spec(r, (PAGE_SIZE, KV_RANK)) for r in range(npg)] +
                [page_spec(r, (QK_ROPE, PAGE_SIZE)) for r in range(npg)] +
                [const(a) for a in consts])
    grid_spec = pltpu.PrefetchScalarGridSpec(
        num_scalar_prefetch=1, grid=(nb, nchunk), in_specs=in_specs,
        out_specs=pl.BlockSpec((1, n_new, MLA_WIDTH), lambda b, c, pt: (b, 0, 0)),
        scratch_shapes=[pltpu.VMEM((_QCOLS, 1), F32), pltpu.VMEM((_QCOLS, 1), F32),
                        pltpu.VMEM((_QCOLS, KV_RANK), F32),
                        pltpu.VMEM((tc, KV_RANK), BF), pltpu.VMEM((QK_ROPE, tc), BF)])
    est = 2 * npg * PAGE_SIZE * (KV_RANK + LANES) * 4 + tc * (KV_RANK + LANES) * 2 + tc * (512 + 4 * LANES) * 4
    return pl.pallas_call(
        functools.partial(_paged_kernel, npg=npg, n_new=n_new), grid_spec=grid_spec,
        out_shape=jax.ShapeDtypeStruct((nb, n_new, MLA_WIDTH), BF),
        compiler_params=pltpu.CompilerParams(dimension_semantics=("arbitrary", "arbitrary"),
                                             vmem_limit_bytes=_vmem_limit(est)),
        name="paged")(page_table, qa, qr, nckv, nkpe, *([cache_ckv] * npg), *([cache_kpe] * npg), *consts)


def _route(logits):
    lane_i = lax.broadcasted_iota(jnp.int32, logits.shape, 1)
    lane = lane_i.astype(F32)
    big = jnp.float32(1 << 20)
    is_g = (lane_i >= N_EXPERTS) & (lane_i < N_EXPERTS + N_GROUPS)
    gl = jnp.where(is_g, logits, NEG)
    gmax = jnp.max(gl, axis=1, keepdims=True)
    gsum = jnp.sum(jnp.where(is_g, jnp.exp(gl - gmax), 0.0), axis=1, keepdims=True)
    p_grp = 1.0 / gsum
    grp = jnp.min(jnp.where(is_g & (gl == gmax), lane, big), axis=1, keepdims=True) - N_EXPERTS
    sel = (lane_i < N_EXPERTS) & ((lane_i // EXPERTS_PER_GROUP).astype(F32) == grp)
    el = jnp.where(sel, logits, NEG)
    emax = jnp.max(el, axis=1, keepdims=True)
    ee = jnp.where(sel, jnp.exp(el - emax), 0.0)
    pe = ee / jnp.sum(ee, axis=1, keepdims=True)
    p1 = jnp.max(pe, axis=1, keepdims=True)
    i1 = jnp.min(jnp.where(sel & (pe == p1), lane, big), axis=1, keepdims=True)
    pe2 = jnp.where(sel & (lane != i1), pe, -1.0)
    p2 = jnp.max(pe2, axis=1, keepdims=True)
    i2 = jnp.min(jnp.where(pe2 == p2, lane, big), axis=1, keepdims=True)
    norm = p_grp / (p1 + p2)
    return jnp.where(lane == i1, p1 * norm, jnp.where(lane == i2, p2 * norm, 0.0))


def _outproj_kernel(x_ref, a_ref, c_ref, m_ref, wo_ref, g2_ref, wrh_ref, wrl_ref, br_ref, x1_out, t_out, gate_out):
    y = _dot(a_ref[...], wo_ref[0:MLA_WIDTH, :])
    y = y + _dot(c_ref[...], wo_ref[MLA_WIDTH:MLA_WIDTH + CONV_WIDTH, :])
    y = y + _dot(m_ref[...], wo_ref[MLA_WIDTH + CONV_WIDTH:, :])
    x1 = x_ref[...] + y
    x1_out[...] = x1
    t = _rms(x1, g2_ref[...])
    th = t.astype(BF)
    t_out[...] = th
    tl = (t - th.astype(F32)).astype(BF)
    logits = _dot(th, wrh_ref[...]) + (_dot(tl, wrh_ref[...]) + _dot(th, wrl_ref[...])) + br_ref[...]
    gate_out[...] = _route(logits)


def _outproj(x, a_mla, a_conv, a_mem, w, tm):
    t = x.shape[0]
    row = lambda i: (i, 0)
    full = lambda i: (0, 0)
    ins = [x, a_mla, a_conv, a_mem, w['w_out'], w['norm2_g'], w['w_router_hi'], w['w_router_lo'], w['b_router']]
    in_specs = [pl.BlockSpec((tm, D_MODEL), row), pl.BlockSpec((tm, MLA_WIDTH), row),
                pl.BlockSpec((tm, CONV_WIDTH), row), pl.BlockSpec((tm, MEM_WIDTH), row)] + \
               [pl.BlockSpec(a.shape, full) for a in ins[4:]]
    out_shape = (jax.ShapeDtypeStruct((t, D_MODEL), F32), jax.ShapeDtypeStruct((t, D_MODEL), BF),
                 jax.ShapeDtypeStruct((t, LANES), F32))
    out_specs = (pl.BlockSpec((tm, D_MODEL), row), pl.BlockSpec((tm, D_MODEL), row), pl.BlockSpec((tm, LANES), row))
    est = 2 * tm * D_MODEL * (4 + 4 + 2 + 2) + 2 * 2 * D_MODEL * D_MODEL + 4 * tm * D_MODEL * 4
    return pl.pallas_call(
        _outproj_kernel, grid=(t // tm,), in_specs=in_specs, out_specs=out_specs, out_shape=out_shape,
        compiler_params=pltpu.CompilerParams(dimension_semantics=("arbitrary",),
                                             vmem_limit_bytes=_vmem_limit(est)),
        name="outproj")(*ins)


def _moe_kernel(t_ref, gate_ref, x1_ref, wg_ref, wu_ref, wd_ref, o_ref):
    e = pl.program_id(1)

    @pl.when(e == 0)
    def _():
        o_ref[...] = x1_ref[...]

    gate = gate_ref[...]
    lane = lax.broadcasted_iota(jnp.int32, gate.shape, 1)
    g = jnp.sum(jnp.where(lane == e, gate, 0.0), axis=1, keepdims=True)
    t = t_ref[...]
    hg = _dot(t, wg_ref[0])
    hu = _dot(t, wu_ref[0])
    h = (hg * _sigmoid(hg) * hu * g).astype(BF)
    o_ref[...] += _dot(h, wd_ref[0])


def _moe(t, gate, x1, w, tm):
    n = t.shape[0]
    row = lambda i, e: (i, 0)
    ex = lambda i, e: (e, 0, 0)
    est = 2 * tm * D_MODEL * (2 + 4 + 4) + 2 * 3 * 2 * D_MODEL * D_EXPERT + 6 * tm * D_EXPERT * 4
    return pl.pallas_call(
        _moe_kernel, grid=(n // tm, N_EXPERTS),
        in_specs=[pl.BlockSpec((tm, D_MODEL), row), pl.BlockSpec((tm, LANES), row), pl.BlockSpec((tm, D_MODEL), row),
                  pl.BlockSpec((1, D_MODEL, D_EXPERT), ex), pl.BlockSpec((1, D_MODEL, D_EXPERT), ex),
                  pl.BlockSpec((1, D_EXPERT, D_MODEL), ex)],
        out_specs=pl.BlockSpec((tm, D_MODEL), row),
        out_shape=jax.ShapeDtypeStruct((n, D_MODEL), F32),
        compiler_params=pltpu.CompilerParams(dimension_semantics=("arbitrary", "arbitrary"),
                                             vmem_limit_bytes=_vmem_limit(est)),
        name="moe")(t, gate, x1, w['w_gate'], w['w_up'], w['w_down'])


def _group_mean_matrix(width, groups):
    m = np.zeros((width, width), np.float32)
    for a, b in groups:
        m[a:b, a:b] = 1.0 / (b - a)
    return jnp.asarray(m, BF)


def _pad_heads(wm, n_heads, width, offset=0, stride=None):
    stride = stride or width
    k = wm.shape[0]
    wm = wm.reshape(k, n_heads, stride)[:, :, offset:offset + width]
    return jnp.pad(wm, ((0, 0), (0, 0), (0, HEAD_PAD - width))).reshape(k, n_heads * HEAD_PAD)


def _lane_row(parts):
    return jnp.concatenate([jnp.asarray(p, F32).reshape(-1) for p in parts])[None, :]


def _layer_weights(l, a):
    w = {}
    w_in = a['w_in'][l]
    cuts = np.cumsum([0, Q_RANK, KV_RANK, QK_ROPE, CONV_WIDTH, CONV_WIDTH, MEM_WIDTH])
    c_q, c_kv, k_r, u_v, u_g, q_m = [w_in[:, cuts[i]:cuts[i + 1]] for i in range(6)]
    k_r = jnp.pad(k_r, ((0, 0), (_ROPE0, LANES - _ROPE0 - QK_ROPE)))
    w['win'] = jnp.concatenate([c_q, c_kv, u_v, u_g, q_m, k_r], axis=1).astype(BF)
    w['n1g'] = a['norm1_g'][l][None, :]
    w['qag'] = a['q_a_norm_g'][l][None, :]
    w['kvg'] = a['kv_a_norm_g'][l][None, :]
    w['wuq'] = _pad_heads(a['w_uq'][l], N_HEADS, QK_NOPE + QK_ROPE).astype(BF)
    w_ukv = a['w_ukv'][l]
    w['wuk'] = _pad_heads(w_ukv, N_HEADS, QK_NOPE, 0, QK_NOPE + V_HEAD).astype(BF)
    w['wuv'] = _pad_heads(w_ukv, N_HEADS, V_HEAD, QK_NOPE, QK_NOPE + V_HEAD).astype(BF)
    zeros = lambda n: np.zeros((n,), np.float32)
    w['gq'] = _lane_row([a['qn_nope_g'][l], a['qn_rope_g'][l], zeros(LANES - 96)]) * (MLA_SCALE * LOG2E)
    w['gk'] = _lane_row([a['kn_nope_g'][l], zeros(LANES - QK_NOPE)])
    w['gkr'] = _lane_row([zeros(_ROPE0), a['kn_rope_g'][l], zeros(LANES - 96)])
    qg, kg = [], []
    for hd in range(N_HEADS):
        qg += [(hd * HEAD_PAD, hd * HEAD_PAD + QK_NOPE), (hd * HEAD_PAD + QK_NOPE, hd * HEAD_PAD + 96)]
        kg += [(hd * HEAD_PAD, hd * HEAD_PAD + QK_NOPE)]
    w['grp_q'] = _group_mean_matrix(N_HEADS * HEAD_PAD, qg)
    w['grp_k'] = _group_mean_matrix(N_HEADS * HEAD_PAD, kg)
    w['grp_m'] = _group_mean_matrix(MEM_WIDTH, [(i * MEM_HEAD_DIM, (i + 1) * MEM_HEAD_DIM) for i in range(MEM_HEADS)])
    w['mqg'] = jnp.tile(a['mem_qn_g'][l], MEM_HEADS)[None, :] * MEM_SCALE
    w['mem_kn_g'] = jnp.tile(a['mem_kn_g'][l], MEM_HEADS)[None, :]
    w['mem_norm_g'] = a['mem_norm_g'][l][None, :]
    w['w_mem_k'] = a['w_mem_k'][l].astype(BF)
    w['w_mem_v'] = a['w_mem_v'][l].astype(BF)
    w['conv_w'] = a['conv_w'][l]
    w['conv_b'] = a['conv_b'][l][None, :]
    w['conv_ln_g'] = a['conv_ln_g'][l][None, :]
    w['conv_ln_b'] = a['conv_ln_b'][l][None, :]
    w['w_out'] = a['w_out'][l].astype(BF)
    w['norm2_g'] = a['norm2_g'][l][None, :]
    w_router = jnp.pad(jnp.concatenate([a['w_router_expert'][l], a['w_router_group'][l]], axis=1),
                       ((0, 0), (0, LANES - N_EXPERTS - N_GROUPS)))
    w['w_router_hi'] = w_router.astype(BF)
    w['w_router_lo'] = (w_router - w['w_router_hi'].astype(F32)).astype(BF)
    w['b_router'] = jnp.pad(jnp.concatenate([a['b_router_expert'][l], a['b_router_group'][l]]),
                            (0, LANES - N_EXPERTS - N_GROUPS))[None, :]
    w['w_gate'] = a['w_gate'][l].astype(BF)
    w['w_up'] = a['w_up'][l].astype(BF)
    w['w_down'] = a['w_down'][l].astype(BF)
    w_uk = w_ukv.reshape(KV_RANK, N_HEADS, QK_NOPE + V_HEAD)[:, :, :QK_NOPE]
    w['wuk_lat'] = jnp.transpose(w_uk, (0, 2, 1)).reshape(KV_RANK, QK_NOPE * N_HEADS).astype(BF)
    w['wuv_lat'] = w_ukv.reshape(KV_RANK, N_HEADS, QK_NOPE + V_HEAD)[:, :, QK_NOPE:].reshape(
        KV_RANK, MLA_WIDTH).astype(BF)
    w['wuk_heads'] = jnp.pad(jnp.transpose(w_uk, (1, 0, 2)), ((0, 0), (0, 0), (0, HEAD_PAD - QK_NOPE)))
    head_of_feat = np.arange(N_HEADS * QK_NOPE) // QK_NOPE
    head_of_qrow = np.arange(_QCOLS) % N_HEADS
    head_of_lane = np.arange(LANES) % N_HEADS
    w['grp_lat'] = jnp.asarray((head_of_qrow[:, None] == head_of_lane[None, :]) / QK_NOPE, BF)
    w['hm_lat'] = jnp.asarray(np.arange(N_HEADS)[:, None] == head_of_feat[None, :], F32)
    return w


def _rope_lane_tables(pos):
    inv_freq = 1.0 / (ROPE_THETA ** (jnp.arange(0, QK_ROPE, 2, dtype=F32) / QK_ROPE))
    ang = pos.astype(F32)[:, None] * inv_freq[None, :]
    cos, sin = jnp.cos(ang), jnp.sin(ang)
    n = pos.shape[0]
    one = jnp.ones((n, _ROPE0), F32)
    zero_hi = jnp.zeros((n, LANES - _ROPE0 - QK_ROPE), F32)
    c = jnp.concatenate([one, cos, cos, zero_hi], axis=1)
    s = jnp.concatenate([jnp.zeros((n, _ROPE0), F32), -sin, sin, zero_hi], axis=1)
    return c, s


def _pick_tile(n, pref):
    return pref if n % pref == 0 else n


def _block_diag_memory(k, v):
    m = k.shape[0]
    hm = (np.arange(MEM_WIDTH)[:, None] // MEM_HEAD_DIM) == (np.arange(MEM_HEADS * m)[None, :] // m)
    kx = jnp.where(hm, jnp.tile(k.T, (1, MEM_HEADS)), 0.0).astype(BF)
    vx = jnp.where(hm.T, jnp.tile(v, (MEM_HEADS, 1)), 0.0).astype(BF)
    return kx, vx


def kernel(x_prompt, x_sample, cache_ckv, cache_kpe, state_conv, cache_mem_k, cache_mem_v, page_table, mem_prompt, norm1_g, w_in, q_a_norm_g, w_uq, kv_a_norm_g, w_ukv, qn_nope_g, qn_rope_g, kn_nope_g, kn_rope_g, conv_w, conv_b, conv_ln_g, conv_ln_b, mem_norm_g, w_mem_k, w_mem_v, mem_qn_g, mem_kn_g, w_out, norm2_g, w_router_group, b_router_group, w_router_expert, b_router_expert, w_gate, w_up, w_down):
    params = dict(norm1_g=norm1_g, w_in=w_in, q_a_norm_g=q_a_norm_g, w_uq=w_uq, kv_a_norm_g=kv_a_norm_g,
                  w_ukv=w_ukv, qn_nope_g=qn_nope_g, qn_rope_g=qn_rope_g, kn_nope_g=kn_nope_g, kn_rope_g=kn_rope_g,
                  conv_w=conv_w, conv_b=conv_b, conv_ln_g=conv_ln_g, conv_ln_b=conv_ln_b, mem_norm_g=mem_norm_g,
                  w_mem_k=w_mem_k, w_mem_v=w_mem_v, mem_qn_g=mem_qn_g, mem_kn_g=mem_kn_g, w_out=w_out,
                  norm2_g=norm2_g, w_router_group=w_router_group, b_router_group=b_router_group,
                  w_router_expert=w_router_expert, b_router_expert=b_router_expert, w_gate=w_gate, w_up=w_up,
                  w_down=w_down)
    depth = w_in.shape[0]
    n_b, n_seq, _ = x_prompt.shape
    assert n_b == 1, "the prompt group is a single sequence"
    n_db, n_dec, _ = x_sample.shape
    n_pages = page_table.shape[1]
    past_len = n_pages * PAGE_SIZE
    n_mem = mem_prompt.shape[1]
    tp = n_b * n_seq
    ts = n_db * n_dec
    assert n_dec <= SUBLANES and n_dec * N_HEADS == _QCOLS

    cos_p, sin_p = _rope_lane_tables(jnp.arange(n_seq))
    cos_s, sin_s = _rope_lane_tables(jnp.tile(past_len + jnp.arange(n_dec), n_db))
    mem_head_mask = jnp.asarray(np.arange(MEM_HEADS)[:, None] == (np.arange(MEM_WIDTH)[None, :] // MEM_HEAD_DIM), F32)
    to_feature_major = lambda c: jnp.transpose(c, (0, 1, 3, 4, 2)).reshape(c.shape[:2] + (MEM_WIDTH, c.shape[2]))
    cache_mk = to_feature_major(cache_mem_k)
    cache_mv = to_feature_major(cache_mem_v)
    cache_kpe_t = jnp.swapaxes(cache_kpe, 2, 3)

    tm_p = _pick_tile(tp, 512)
    tq = _pick_tile(tp, 1024)
    tm_moe = _pick_tile(tp, 1024)
    npg = _PAGES_PER_STEP if n_pages % _PAGES_PER_STEP == 0 else n_pages
    nseq_mem = 8 if n_db % 8 == 0 else n_db

    xp = x_prompt.reshape(tp, D_MODEL)
    xs = x_sample.reshape(ts, D_MODEL)
    outs = {k: [] for k in ('ckv_p', 'kpe_p', 'conv_p', 'memk_p', 'memv_p', 'ckv_s', 'kpe_s', 'conv_s')}
    for l in range(depth):
        w = _layer_weights(l, params)

        mk, mv = _memkv(mem_prompt.reshape(n_mem, D_MODEL), w)
        q, k, v, ckv, kpe, glu, qm = _premix(xp, w, cos_p, sin_p, tm_p)
        a_mla = _flash_attention(q, k, v, tq)
        padded = jnp.concatenate([jnp.zeros((_CONV_HALO, CONV_WIDTH), F32), glu], axis=0)
        a_conv = _conv_rows(padded, w, tm_p)
        kx, vx = _block_diag_memory(mk, mv)
        a_mem = _memattn_rows(qm, kx, vx, tm_p)
        x1, t, gate = _outproj(xp, a_mla, a_conv, a_mem, w, tm_p)
        xp = _moe(t, gate, x1, w, tm_moe)
        outs['ckv_p'].append(ckv.reshape(n_b, n_seq, KV_RANK))
        outs['kpe_p'].append(kpe.reshape(n_b, n_seq, QK_ROPE))
        outs['conv_p'].append(padded[-(CONV_K - 1):].reshape(n_b, CONV_K - 1, CONV_WIDTH))
        outs['memk_p'].append(mk.reshape(n_b, n_mem, MEM_HEADS, MEM_HEAD_DIM))
        outs['memv_p'].append(mv.reshape(n_b, n_mem, MEM_HEADS, MEM_HEAD_DIM))

        q, k, v, ckv, kpe, glu, qm = _premix(xs, w, cos_s, sin_s, _pick_tile(ts, 512))
        qlat = _absorb(q, w['gk'], w['wuk_heads'])
        per_seq_rows = lambda a: jnp.transpose(a.reshape(N_HEADS, a.shape[1], n_db, n_dec),
                                               (2, 3, 0, 1)).reshape(n_db, _QCOLS, a.shape[1])
        qa = per_seq_rows(qlat)
        qr = per_seq_rows(q[:, _ROPE0:_ROPE0 + QK_ROPE, :])
        new_ckv = jnp.pad(ckv.reshape(n_db, n_dec, KV_RANK), ((0, 0), (0, 2 * SUBLANES - n_dec), (0, 0)))
        new_kpe_t = jnp.pad(jnp.swapaxes(kpe.reshape(n_db, n_dec, QK_ROPE), 1, 2),
                            ((0, 0), (0, 0), (0, PAGE_SIZE - n_dec)))
        a_mla = _paged_attention(page_table, qa, qr, new_ckv, new_kpe_t,
                                 cache_ckv, cache_kpe_t, l, w, npg, n_dec).reshape(ts, MLA_WIDTH)
        padded = jnp.concatenate([state_conv[l], glu.reshape(n_db, n_dec, CONV_WIDTH)], axis=1)
        a_conv = jnp.transpose(_conv_steps(jnp.transpose(padded, (1, 0, 2)), w), (1, 0, 2)).reshape(ts, CONV_WIDTH)
        q8 = jnp.pad(qm.reshape(n_db, n_dec, MEM_WIDTH), ((0, 0), (0, SUBLANES - n_dec), (0, 0)))
        a_mem = _memattn_seqs(q8, cache_mk, cache_mv, l, mem_head_mask, nseq_mem)[:, :n_dec].reshape(ts, MEM_WIDTH)
        x1, t, gate = _outproj(xs, a_mla, a_conv, a_mem, w, _pick_tile(ts, 512))
        xs = _moe(t, gate, x1, w, _pick_tile(ts, 512))
        outs['ckv_s'].append(ckv.reshape(n_db, n_dec, KV_RANK))
        outs['kpe_s'].append(kpe.reshape(n_db, n_dec, QK_ROPE))
        outs['conv_s'].append(padded[:, -(CONV_K - 1):])

    st = lambda key: jnp.stack(outs[key])
    return (xp.reshape(n_b, n_seq, D_MODEL), xs.reshape(n_db, n_dec, D_MODEL), st('ckv_p'), st('kpe_p'),
            st('conv_p'), st('memk_p'), st('memv_p'), st('ckv_s'), st('kpe_s'), st('conv_s'))
```

```python
import functools

import numpy as np
import jax
import jax.numpy as jnp
from jax import lax
from jax.experimental import pallas as pl
from jax.experimental.pallas import tpu as pltpu

D_MODEL = 1024
N_HEADS = 8
QK_NOPE = 64
QK_ROPE = 32
V_HEAD = 64
Q_RANK = 384
KV_RANK = 256
MLA_WIDTH = N_HEADS * V_HEAD
CONV_WIDTH = 256
MEM_WIDTH = 256
CONV_K = 31
MEM_HEADS = 4
MEM_HEAD_DIM = 64
N_GROUPS = 4
EXPERTS_PER_GROUP = 8
N_EXPERTS = 32
D_EXPERT = 256
PAGE_SIZE = 128
ROPE_THETA = 10000.0
EPS = 1e-6
MLA_SCALE = (QK_NOPE + QK_ROPE) ** -0.5
MEM_SCALE = MEM_HEAD_DIM ** -0.5

LANES = 128
SUBLANES = 8
VMEM_BYTES = 64 << 20
HEAD_PAD = LANES

_CQ0, _CKV0, _UV0, _UG0, _QM0, _KR0, _IN_PAD = 0, 384, 640, 896, 1152, 1408, 1536
_ROPE0 = QK_NOPE
_HALF = QK_ROPE // 2
NEG = -1e30
LOG2E = 1.4426950408889634
V_ROWS = 80

BF = jnp.bfloat16
F32 = jnp.float32


def _vmem_limit(nbytes):
    return int(min(max(2 * nbytes, 16 << 20), VMEM_BYTES - (8 << 20)))


def _dot(a, b):
    return jnp.dot(a, b, preferred_element_type=F32)


def _dot_nt(a, b):
    return lax.dot_general(a, b, (((1,), (1,)), ((), ())), preferred_element_type=F32)


def _rms(x, g):
    return x * lax.rsqrt(jnp.mean(x * x, axis=-1, keepdims=True) + EPS) * g


def _sigmoid(x):
    return 1.0 / (1.0 + jnp.exp(-x))


def _rope_padded(x, c, s):
    lane = lax.broadcasted_iota(jnp.int32, x.shape, 1)
    swapped = jnp.where(lane < _ROPE0 + _HALF, pltpu.roll(x, LANES - _HALF, 1), pltpu.roll(x, _HALF, 1))
    return x * c + swapped * s


def _premix_kernel(x_ref, n1g_ref, win_ref, qag_ref, wuq_ref, kvg_ref, wuk_ref, wuv_ref, gq_ref, gk_ref,
                   gkr_ref, grp_q_ref, grp_k_ref, grp_m_ref, mqg_ref, cos_ref, sin_ref,
                   q_out, k_out, v_out, ckv_out, kpe_out, glu_out, qm_out):
    x = x_ref[...]
    h = _rms(x, n1g_ref[...]).astype(BF)

    def seg(a, b):
        return _dot(h, win_ref[:, a:b])

    c = cos_ref[...]
    s = sin_ref[...]

    cq = _rms(seg(_CQ0, _CKV0), qag_ref[...]).astype(BF)
    q = _dot(cq, wuq_ref[...])
    q = q * lax.rsqrt(_dot((q * q).astype(BF), grp_q_ref[...]) + EPS)
    gq = gq_ref[...]
    for hd in range(N_HEADS):
        qh = q[:, hd * HEAD_PAD:(hd + 1) * HEAD_PAD] * gq
        q_out[hd] = _rope_padded(qh, c, s).T.astype(BF)

    ckv = _rms(seg(_CKV0, _UV0), kvg_ref[...])
    ckv_out[...] = ckv
    ckvb = ckv.astype(BF)
    kr = seg(_KR0, _IN_PAD)
    kr = kr * lax.rsqrt(jnp.sum(kr * kr, axis=-1, keepdims=True) * (1.0 / QK_ROPE) + EPS) * gkr_ref[...]
    kpe = _rope_padded(kr, c, s)
    kpe_out[...] = kpe[:, _ROPE0:_ROPE0 + QK_ROPE]

    kx = _dot(ckvb, wuk_ref[...])
    kx = kx * lax.rsqrt(_dot((kx * kx).astype(BF), grp_k_ref[...]) + EPS)
    vx = _dot(ckvb, wuv_ref[...])
    gk = gk_ref[...]
    ones_row = lax.broadcasted_iota(jnp.int32, (V_ROWS, x.shape[0]), 0) == V_HEAD
    for hd in range(N_HEADS):
        sl = slice(hd * HEAD_PAD, (hd + 1) * HEAD_PAD)
        k_out[hd] = (kx[:, sl] * gk + kpe).astype(BF)
        v_out[hd] = jnp.where(ones_row, 1.0, vx[:, sl].T[:V_ROWS]).astype(BF)

    glu_out[...] = seg(_UV0, _UG0) * _sigmoid(seg(_UG0, _QM0))
    qm = seg(_QM0, _KR0)
    qm = qm * lax.rsqrt(_dot((qm * qm).astype(BF), grp_m_ref[...]) + EPS) * mqg_ref[...]
    qm_out[...] = qm.astype(BF)


def _premix(x, w, cos_t, sin_t, tm):
    t = x.shape[0]
    nt = t // tm
    row = lambda i: (i, 0)
    full = lambda i: (0, 0)
    hm = lambda i: (0, i, 0)
    wspec = lambda a: pl.BlockSpec(a.shape, full)
    ins = [x, w['n1g'], w['win'], w['qag'], w['wuq'], w['kvg'], w['wuk'], w['wuv'], w['gq'], w['gk'], w['gkr'],
           w['grp_q'], w['grp_k'], w['grp_m'], w['mqg'], cos_t, sin_t]
    in_specs = [pl.BlockSpec((tm, D_MODEL), row)] + [wspec(a) for a in ins[1:15]] + \
               [pl.BlockSpec((tm, LANES), row), pl.BlockSpec((tm, LANES), row)]
    fm = lambda i: (0, 0, i)
    out_shape = (jax.ShapeDtypeStruct((N_HEADS, HEAD_PAD, t), BF), jax.ShapeDtypeStruct((N_HEADS, t, HEAD_PAD), BF),
                 jax.ShapeDtypeStruct((N_HEADS, V_ROWS, t), BF),
                 jax.ShapeDtypeStruct((t, KV_RANK), F32), jax.ShapeDtypeStruct((t, QK_ROPE), F32),
                 jax.ShapeDtypeStruct((t, CONV_WIDTH), F32), jax.ShapeDtypeStruct((t, MEM_WIDTH), BF))
    out_specs = (pl.BlockSpec((N_HEADS, HEAD_PAD, tm), fm), pl.BlockSpec((N_HEADS, tm, HEAD_PAD), hm),
                 pl.BlockSpec((N_HEADS, V_ROWS, tm), fm),
                 pl.BlockSpec((tm, KV_RANK), row), pl.BlockSpec((tm, QK_ROPE), row),
                 pl.BlockSpec((tm, CONV_WIDTH), row), pl.BlockSpec((tm, MEM_WIDTH), row))
    wbytes = sum(int(np.prod(a.shape)) * a.dtype.itemsize for a in ins[1:15])
    est = 2 * wbytes + tm * (2 * 4 * D_MODEL + 3 * 2 * 2 * N_HEADS * HEAD_PAD + 8 * 4 * N_HEADS * HEAD_PAD)
    return pl.pallas_call(
        _premix_kernel, grid=(nt,), in_specs=in_specs, out_specs=out_specs, out_shape=out_shape,
        compiler_params=pltpu.CompilerParams(dimension_semantics=("arbitrary",),
                                             vmem_limit_bytes=_vmem_limit(est)),
        name="premix")(*ins)


_EXP2_HEADROOM = 60.0


def _flash_kernel(qi_ref, kj_ref, qt_ref, k_ref, vt_ref, o_ref, m_scr, acc_scr):
    p = pl.program_id(0)
    i = qi_ref[p]
    j = kj_ref[p]

    def scores(hd, masked):
        st = _dot(k_ref[hd], qt_ref[hd])
        if masked:
            kk = lax.broadcasted_iota(jnp.int32, st.shape, 0)
            qq = lax.broadcasted_iota(jnp.int32, st.shape, 1)
            st = jnp.where(kk <= qq, st, NEG)
        return st

    def sweep(masked, first):
        def body(hd, carry):
            st = scores(hd, masked)
            if first:
                m = jnp.max(st, axis=0, keepdims=True)
                acc_scr[hd] = _dot(vt_ref[hd], jnp.exp2(st - m).astype(BF))
                m_scr[hd] = m
                return carry
            m_ref = m_scr[hd]
            pt = jnp.exp2(st - m_ref).astype(BF)
            gap = jnp.max(jnp.max(st, axis=0, keepdims=True) - m_ref)

            @pl.when(gap <= _EXP2_HEADROOM)
            def _():
                acc_scr[hd] += _dot(vt_ref[hd], pt)

            @pl.when(gap > _EXP2_HEADROOM)
            def _():
                st2 = scores(hd, masked)
                m_new = jnp.maximum(m_ref, jnp.max(st2, axis=0, keepdims=True))
                acc_scr[hd] = jnp.exp2(m_ref - m_new) * acc_scr[hd] + _dot(vt_ref[hd], jnp.exp2(st2 - m_new).astype(BF))
                m_scr[hd] = m_new
            return carry
        lax.fori_loop(0, N_HEADS, body, 0)

    @pl.when((j == 0) & (i > 0))
    def _():
        sweep(False, True)

    @pl.when((j > 0) & (j < i))
    def _():
        sweep(False, False)

    @pl.when((j == i) & (i == 0))
    def _():
        sweep(True, True)

    @pl.when((j == i) & (i > 0))
    def _():
        sweep(True, False)

    @pl.when(j == i)
    def _():
        for hp in range(N_HEADS // 2):
            halves = []
            for hd in (2 * hp, 2 * hp + 1):
                acc = acc_scr[hd]
                halves.append((acc[:V_HEAD] / acc[V_HEAD:V_HEAD + 1]).T)
            o_ref[:, hp * LANES:(hp + 1) * LANES] = jnp.concatenate(halves, axis=1).astype(o_ref.dtype)


def _flash_attention(qt, k, vt, tq):
    t = k.shape[1]
    nq = t // tq
    qi = np.concatenate([np.full(i + 1, i, np.int32) for i in range(nq)])
    kj = np.concatenate([np.arange(i + 1, dtype=np.int32) for i in range(nq)])
    grid_spec = pltpu.PrefetchScalarGridSpec(
        num_scalar_prefetch=2, grid=(len(qi),),
        in_specs=[pl.BlockSpec((N_HEADS, HEAD_PAD, tq), lambda p, qi, kj: (0, 0, qi[p])),
                  pl.BlockSpec((N_HEADS, tq, HEAD_PAD), lambda p, qi, kj: (0, kj[p], 0)),
                  pl.BlockSpec((N_HEADS, V_ROWS, tq), lambda p, qi, kj: (0, 0, kj[p]))],
        out_specs=pl.BlockSpec((tq, MLA_WIDTH), lambda p, qi, kj: (qi[p], 0)),
        scratch_shapes=[pltpu.VMEM((N_HEADS, 1, tq), F32), pltpu.VMEM((N_HEADS, V_ROWS, tq), F32)])
    nblk = N_HEADS * tq * HEAD_PAD
    est = 3 * 2 * 2 * nblk + 4 * nblk + 2 * 2 * tq * MLA_WIDTH + 3 * 4 * tq * tq
    return pl.pallas_call(
        _flash_kernel, grid_spec=grid_spec,
        out_shape=jax.ShapeDtypeStruct((t, MLA_WIDTH), BF),
        compiler_params=pltpu.CompilerParams(dimension_semantics=("arbitrary",),
                                             vmem_limit_bytes=_vmem_limit(est)),
        name="flash")(jnp.asarray(qi), jnp.asarray(kj), qt, k, vt)


_CONV_HALO = 32
_CONV_LEAD = _CONV_HALO - (CONV_K - 1)

def _ln_swish(y, g, b):
    mu = jnp.mean(y, axis=-1, keepdims=True)
    d = y - mu
    y = d * lax.rsqrt(jnp.mean(d * d, axis=-1, keepdims=True) + EPS) * g + b
    return y * _sigmoid(y)


def _conv_rows_kernel(main_ref, halo_ref, w_ref, b_ref, g_ref, lb_ref, o_ref, buf, *, tm):
    buf[0:tm, :] = main_ref[...]
    buf[tm:tm + _CONV_HALO, :] = halo_ref[...]
    acc = jnp.zeros((tm, CONV_WIDTH), F32)
    for j in range(CONV_K):
        acc = acc + buf[_CONV_LEAD + j:_CONV_LEAD + j + tm, :] * w_ref[j:j + 1, :]
    o_ref[...] = _ln_swish(acc + b_ref[...], g_ref[...], lb_ref[...]).astype(o_ref.dtype)


def _conv_rows(padded, w, tm):
    t = padded.shape[0] - _CONV_HALO
    full = lambda i: (0, 0)
    hb = tm // _CONV_HALO
    return pl.pallas_call(
        functools.partial(_conv_rows_kernel, tm=tm), grid=(t // tm,),
        in_specs=[pl.BlockSpec((tm, CONV_WIDTH), lambda i: (i, 0)),
                  pl.BlockSpec((_CONV_HALO, CONV_WIDTH), lambda i: ((i + 1) * hb, 0)),
                  pl.BlockSpec((CONV_K, CONV_WIDTH), full)] + [pl.BlockSpec((1, CONV_WIDTH), full)] * 3,
        out_specs=pl.BlockSpec((tm, CONV_WIDTH), lambda i: (i, 0)),
        out_shape=jax.ShapeDtypeStruct((t, CONV_WIDTH), BF),
        scratch_shapes=[pltpu.VMEM((tm + _CONV_HALO, CONV_WIDTH), F32)],
        compiler_params=pltpu.CompilerParams(dimension_semantics=("arbitrary",)),
        name="conv_rows")(padded, padded, w['conv_w'], w['conv_b'], w['conv_ln_g'], w['conv_ln_b'])


def _conv_steps_kernel(p_ref, w_ref, b_ref, g_ref, lb_ref, o_ref, *, n_new):
    for t in range(n_new):
        acc = jnp.zeros(o_ref.shape[1:], F32)
        for j in range(CONV_K):
            acc = acc + p_ref[t + j] * w_ref[j:j + 1, :]
        o_ref[t] = _ln_swish(acc + b_ref[...], g_ref[...], lb_ref[...]).astype(o_ref.dtype)


def _conv_steps(padded_t, w):
    n_new = padded_t.shape[0] - (CONV_K - 1)
    nb = padded_t.shape[1]
    full2 = lambda i: (0, 0)
    full3 = lambda i: (0, 0, 0)
    return pl.pallas_call(
        functools.partial(_conv_steps_kernel, n_new=n_new), grid=(1,),
        in_specs=[pl.BlockSpec(padded_t.shape, full3), pl.BlockSpec((CONV_K, CONV_WIDTH), full2)] +
                 [pl.BlockSpec((1, CONV_WIDTH), full2)] * 3,
        out_specs=pl.BlockSpec((n_new, nb, CONV_WIDTH), full3),
        out_shape=jax.ShapeDtypeStruct((n_new, nb, CONV_WIDTH), BF),
        name="conv_steps")(padded_t, w['conv_w'], w['conv_b'], w['conv_ln_g'], w['conv_ln_b'])


def _memkv_kernel(mem_ref, g_ref, wk_ref, wv_ref, grp_ref, kg_ref, k_out, v_out):
    m = _rms(mem_ref[...], g_ref[...]).astype(BF)
    k = _dot(m, wk_ref[...])
    k_out[...] = k * lax.rsqrt(_dot((k * k).astype(BF), grp_ref[...]) + EPS) * kg_ref[...]
    v_out[...] = _dot(m, wv_ref[...])


def _memkv(mem, w):
    n = mem.shape[0]
    full = lambda i: (0, 0)
    ins = [mem, w['mem_norm_g'], w['w_mem_k'], w['w_mem_v'], w['grp_m'], w['mem_kn_g']]
    o = jax.ShapeDtypeStruct((n, MEM_WIDTH), F32)
    return pl.pallas_call(
        _memkv_kernel, grid=(1,), in_specs=[pl.BlockSpec(a.shape, full) for a in ins],
        out_specs=(pl.BlockSpec((n, MEM_WIDTH), full),) * 2, out_shape=(o, o),
        name="memkv")(*ins)


def _memattn_rows_kernel(q_ref, kx_ref, vx_ref, o_ref, *, n_mem):
    s = _dot(q_ref[...], kx_ref[...])
    parts = []
    for hd in range(MEM_HEADS):
        sh = s[:, hd * n_mem:(hd + 1) * n_mem]
        e = jnp.exp(sh - jnp.max(sh, axis=1, keepdims=True))
        parts.append((e / jnp.sum(e, axis=1, keepdims=True)).astype(BF))
    o_ref[...] = _dot(jnp.concatenate(parts, axis=1), vx_ref[...]).astype(o_ref.dtype)


def _memattn_rows(qm, kx, vx, tm):
    t = qm.shape[0]
    n_mem = vx.shape[0] // MEM_HEADS
    full = lambda i: (0, 0)
    return pl.pallas_call(
        functools.partial(_memattn_rows_kernel, n_mem=n_mem), grid=(t // tm,),
        in_specs=[pl.BlockSpec((tm, MEM_WIDTH), lambda i: (i, 0)), pl.BlockSpec(kx.shape, full),
                  pl.BlockSpec(vx.shape, full)],
        out_specs=pl.BlockSpec((tm, MEM_WIDTH), lambda i: (i, 0)),
        out_shape=jax.ShapeDtypeStruct((t, MEM_WIDTH), BF),
        compiler_params=pltpu.CompilerParams(dimension_semantics=("arbitrary",)),
        name="memattn_rows")(qm, kx, vx)


def _memattn_seqs_kernel(q_ref, k_ref, v_ref, hm_ref, o_ref, *, nseq):
    hm = hm_ref[...]
    for b in range(nseq):
        q8 = q_ref[b].astype(F32)
        qe = jnp.concatenate([q8 * hm[hd:hd + 1, :] for hd in range(MEM_HEADS)], axis=0).astype(BF)
        s = _dot(qe, k_ref[0, b].astype(BF))
        e = jnp.exp(s - jnp.max(s, axis=1, keepdims=True))
        pr = (e / jnp.sum(e, axis=1, keepdims=True)).astype(BF)
        o = _dot_nt(pr, v_ref[0, b].astype(BF))
        acc = o[0:SUBLANES] * hm[0:1, :]
        for hd in range(1, MEM_HEADS):
            acc = acc + o[hd * SUBLANES:(hd + 1) * SUBLANES] * hm[hd:hd + 1, :]
        o_ref[b] = acc.astype(o_ref.dtype)


def _memattn_seqs(q8, cache_k, cache_v, layer, head_mask, nseq):
    nb = q8.shape[0]
    m = cache_k.shape[3]
    return pl.pallas_call(
        functools.partial(_memattn_seqs_kernel, nseq=nseq), grid=(nb // nseq,),
        in_specs=[pl.BlockSpec((nseq, SUBLANES, MEM_WIDTH), lambda i: (i, 0, 0)),
                  pl.BlockSpec((1, nseq, MEM_WIDTH, m), lambda i: (layer, i, 0, 0)),
                  pl.BlockSpec((1, nseq, MEM_WIDTH, m), lambda i: (layer, i, 0, 0)),
                  pl.BlockSpec(head_mask.shape, lambda i: (0, 0))],
        out_specs=pl.BlockSpec((nseq, SUBLANES, MEM_WIDTH), lambda i: (i, 0, 0)),
        out_shape=jax.ShapeDtypeStruct((nb, SUBLANES, MEM_WIDTH), BF),
        compiler_params=pltpu.CompilerParams(dimension_semantics=("arbitrary",)),
        name="memattn_seqs")(q8, cache_k, cache_v, head_mask)


_QCOLS = 32
_PAGES_PER_STEP = 32


def _absorb_kernel(qt_ref, gk_ref, wuk_ref, o_ref):
    for hd in range(N_HEADS):
        wg = (wuk_ref[hd] * gk_ref[...]).astype(BF)
        o_ref[hd] = _dot(wg, qt_ref[hd]).astype(o_ref.dtype)


def _absorb(qt, gk, wuk):
    t = qt.shape[2]
    f3 = lambda i: (0, 0, 0)
    return pl.pallas_call(
        _absorb_kernel, grid=(1,),
        in_specs=[pl.BlockSpec(qt.shape, f3), pl.BlockSpec(gk.shape, lambda i: (0, 0)), pl.BlockSpec(wuk.shape, f3)],
        out_specs=pl.BlockSpec((N_HEADS, KV_RANK, t), f3),
        out_shape=jax.ShapeDtypeStruct((N_HEADS, KV_RANK, t), BF),
        name="absorb")(qt, gk, wuk)


def _paged_kernel(pt_ref, qa_ref, qr_ref, nckv_ref, nkpe_ref, ckv_hbm, kpe_hbm, wuk_ref, grp_ref, wuv_ref, hm_ref,
                  o_ref, m_scr, l_scr, acc_scr, ckv_scr, kpe_scr, ckv_buf, kpe_buf, sems, *, npg, n_new, layer):
    c = pl.program_id(1)
    nchunk = pl.num_programs(1)
    step = pl.program_id(0) * nchunk + c
    nsteps = pl.num_programs(0) * nchunk
    slot = step % 2

    def page_copies(s, slot_):
        for r in range(npg):
            page = pt_ref[s * npg + r]
            yield pltpu.make_async_copy(ckv_hbm.at[layer, page], ckv_buf.at[slot_, r], sems.at[0, slot_])
            yield pltpu.make_async_copy(kpe_hbm.at[layer, page], kpe_buf.at[slot_, r], sems.at[1, slot_])

    @pl.when(step == 0)
    def _():
        for cp in page_copies(0, 0):
            cp.start()

    @pl.when(step + 1 < nsteps)
    def _():
        for cp in page_copies(step + 1, 1 - slot):
            cp.start()

    for cp in page_copies(step, slot):
        cp.wait()

    qa = qa_ref[0]
    qr = qr_ref[0]

    def key_energy(ckvb):
        kn = _dot(ckvb, wuk_ref[...])
        k2 = kn * kn
        return ((k2[:, 0:LANES] + k2[:, LANES:2 * LANES]) + (k2[:, 2 * LANES:3 * LANES] + k2[:, 3 * LANES:])).astype(BF)

    def attend(ckvb, kpet, causal):
        r = lax.rsqrt(_dot_nt(grp_ref[...], key_energy(ckvb)) + EPS)
        st = _dot_nt(qa, ckvb) * r + _dot(qr, kpet)
        if causal:
            row = lax.broadcasted_iota(jnp.int32, st.shape, 0)
            col = lax.broadcasted_iota(jnp.int32, st.shape, 1)
            st = jnp.where(col <= row // N_HEADS, st, NEG)
        m = jnp.max(st, axis=1, keepdims=True)
        pr = jnp.exp2(st - m)
        pb = pr.astype(BF)
        half = ckvb.shape[0] // 2
        pv = _dot(pb[:, :half], ckvb[:half]) + _dot(pb[:, half:], ckvb[half:])
        return m, jnp.sum(pr, axis=1, keepdims=True), pv

    def merge(parts):
        m_all = functools.reduce(jnp.maximum, [p[0] for p in parts])
        scales = [jnp.exp2(p[0] - m_all) for p in parts]
        m_scr[...] = m_all
        l_scr[...] = sum(s * p[1] for s, p in zip(scales, parts))
        acc_scr[...] = sum(s * p[2] for s, p in zip(scales, parts))

    @pl.when(c == 0)
    def _():
        ckv_scr[0:PAGE_SIZE, :] = jnp.zeros((PAGE_SIZE, KV_RANK), BF)
        ckv_scr[0:2 * SUBLANES, :] = nckv_ref[0].astype(BF)
        merge([attend(ckv_scr[0:PAGE_SIZE, :], nkpe_ref[0].astype(BF), True)])

    for r in range(npg):
        ckv_scr[r * PAGE_SIZE:(r + 1) * PAGE_SIZE, :] = ckv_buf[slot, r].astype(BF)
        kpe_scr[:, r * PAGE_SIZE:(r + 1) * PAGE_SIZE] = kpe_buf[slot, r].astype(BF)
    merge([(m_scr[...], l_scr[...], acc_scr[...]), attend(ckv_scr[...], kpe_scr[...], False)])

    @pl.when(c == pl.num_programs(1) - 1)
    def _():
        lat = (acc_scr[...] / l_scr[...]).astype(BF)
        full = _dot(lat, wuv_ref[...])
        hm = hm_ref[...]
        rows = [jnp.sum(full[qi * N_HEADS:(qi + 1) * N_HEADS] * hm, axis=0, keepdims=True)
                for qi in range(n_new)]
        o_ref[0] = jnp.concatenate(rows, axis=0).astype(o_ref.dtype)


def _paged_attention(page_table, qa, qr, nckv, nkpe, cache_ckv, cache_kpe, layer, w, npg, n_new):
    nb, n_pages = page_table.shape
    nchunk = n_pages // npg
    tc = npg * PAGE_SIZE

    per_seq = lambda shp: pl.BlockSpec((1,) + shp, lambda b, c, pt: (b, 0, 0))
    const = lambda a: pl.BlockSpec(a.shape, lambda b, c, pt: (0, 0))
    hbm = pl.BlockSpec(memory_space=pl.ANY)
    consts = [w['wuk_lat'], w['grp_lat'], w['wuv_lat'], w['hm_lat']]
    in_specs = ([per_seq((_QCOLS, KV_RANK)), per_seq((_QCOLS, QK_ROPE)), per_seq((2 * SUBLANES, KV_RANK)),
                 per_seq((QK_ROPE, PAGE_SIZE)), hbm, hbm] + [const(a) for a in consts])
    grid_spec = pltpu.PrefetchScalarGridSpec(
        num_scalar_prefetch=1, grid=(nb, nchunk), in_specs=in_specs,
        out_specs=pl.BlockSpec((1, n_new, MLA_WIDTH), lambda b, c, pt: (b, 0, 0)),
        scratch_shapes=[pltpu.VMEM((_QCOLS, 1), F32), pltpu.VMEM((_QCOLS, 1), F32),
                        pltpu.VMEM((_QCOLS, KV_RANK), F32),
                        pltpu.VMEM((tc, KV_RANK), BF), pltpu.VMEM((QK_ROPE, tc), BF),
                        pltpu.VMEM((2, npg, PAGE_SIZE, KV_RANK), F32), pltpu.VMEM((2, npg, QK_ROPE, PAGE_SIZE), F32),
                        pltpu.SemaphoreType.DMA((2, 2))])
    est = 2 * npg * PAGE_SIZE * (KV_RANK + QK_ROPE) * 4 + tc * (KV_RANK + LANES) * 2 + tc * (512 + 4 * LANES) * 4
    return pl.pallas_call(
        functools.partial(_paged_kernel, npg=npg, n_new=n_new, layer=layer), grid_spec=grid_spec,
        out_shape=jax.ShapeDtypeStruct((nb, n_new, MLA_WIDTH), BF),
        compiler_params=pltpu.CompilerParams(dimension_semantics=("arbitrary", "arbitrary"),
                                             vmem_limit_bytes=_vmem_limit(est)),
        name="paged")(page_table.reshape(-1), qa, qr, nckv, nkpe, cache_ckv, cache_kpe, *consts)


def _route(logits):
    lane_i = lax.broadcasted_iota(jnp.int32, logits.shape, 1)
    lane = lane_i.astype(F32)
    big = jnp.float32(1 << 20)
    is_g = (lane_i >= N_EXPERTS) & (lane_i < N_EXPERTS + N_GROUPS)
    gl = jnp.where(is_g, logits, NEG)
    gmax = jnp.max(gl, axis=1, keepdims=True)
    gsum = jnp.sum(jnp.where(is_g, jnp.exp(gl - gmax), 0.0), axis=1, keepdims=True)
    p_grp = 1.0 / gsum
    grp = jnp.min(jnp.where(is_g & (gl == gmax), lane, big), axis=1, keepdims=True) - N_EXPERTS
    sel = (lane_i < N_EXPERTS) & ((lane_i // EXPERTS_PER_GROUP).astype(F32) == grp)
    el = jnp.where(sel, logits, NEG)
    emax = jnp.max(el, axis=1, keepdims=True)
    ee = jnp.where(sel, jnp.exp(el - emax), 0.0)
    pe = ee / jnp.sum(ee, axis=1, keepdims=True)
    p1 = jnp.max(pe, axis=1, keepdims=True)
    i1 = jnp.min(jnp.where(sel & (pe == p1), lane, big), axis=1, keepdims=True)
    pe2 = jnp.where(sel & (lane != i1), pe, -1.0)
    p2 = jnp.max(pe2, axis=1, keepdims=True)
    i2 = jnp.min(jnp.where(pe2 == p2, lane, big), axis=1, keepdims=True)
    norm = p_grp / (p1 + p2)
    return jnp.where(lane == i1, p1 * norm, jnp.where(lane == i2, p2 * norm, 0.0))


def _outproj_kernel(x_ref, a_ref, c_ref, m_ref, wo_ref, g2_ref, wrh_ref, wrl_ref, br_ref, x1_out, t_out, gate_out):
    y = _dot(a_ref[...], wo_ref[0:MLA_WIDTH, :])
    y = y + _dot(c_ref[...], wo_ref[MLA_WIDTH:MLA_WIDTH + CONV_WIDTH, :])
    y = y + _dot(m_ref[...], wo_ref[MLA_WIDTH + CONV_WIDTH:, :])
    x1 = x_ref[...] + y
    x1_out[...] = x1
    t = _rms(x1, g2_ref[...])
    th = t.astype(BF)
    t_out[...] = th
    tl = (t - th.astype(F32)).astype(BF)
    logits = _dot(th, wrh_ref[...]) + (_dot(tl, wrh_ref[...]) + _dot(th, wrl_ref[...])) + br_ref[...]
    gate_out[...] = _route(logits)


def _outproj(x, a_mla, a_conv, a_mem, w, tm):
    t = x.shape[0]
    row = lambda i: (i, 0)
    full = lambda i: (0, 0)
    ins = [x, a_mla, a_conv, a_mem, w['w_out'], w['norm2_g'], w['w_router_hi'], w['w_router_lo'], w['b_router']]
    in_specs = [pl.BlockSpec((tm, D_MODEL), row), pl.BlockSpec((tm, MLA_WIDTH), row),
                pl.BlockSpec((tm, CONV_WIDTH), row), pl.BlockSpec((tm, MEM_WIDTH), row)] + \
               [pl.BlockSpec(a.shape, full) for a in ins[4:]]
    out_shape = (jax.ShapeDtypeStruct((t, D_MODEL), F32), jax.ShapeDtypeStruct((t, D_MODEL), BF),
                 jax.ShapeDtypeStruct((t, LANES), F32))
    out_specs = (pl.BlockSpec((tm, D_MODEL), row), pl.BlockSpec((tm, D_MODEL), row), pl.BlockSpec((tm, LANES), row))
    est = 2 * tm * D_MODEL * (4 + 4 + 2 + 2) + 2 * 2 * D_MODEL * D_MODEL + 4 * tm * D_MODEL * 4
    return pl.pallas_call(
        _outproj_kernel, grid=(t // tm,), in_specs=in_specs, out_specs=out_specs, out_shape=out_shape,
        compiler_params=pltpu.CompilerParams(dimension_semantics=("arbitrary",),
                                             vmem_limit_bytes=_vmem_limit(est)),
        name="outproj")(*ins)


def _moe_kernel(t_ref, gate_ref, x1_ref, wg_ref, wu_ref, wd_ref, o_ref):
    e = pl.program_id(1)

    @pl.when(e == 0)
    def _():
        o_ref[...] = x1_ref[...]

    gate = gate_ref[...]
    lane = lax.broadcasted_iota(jnp.int32, gate.shape, 1)
    g = jnp.sum(jnp.where(lane == e, gate, 0.0), axis=1, keepdims=True)
    t = t_ref[...]
    hg = _dot(t, wg_ref[0].astype(BF))
    hu = _dot(t, wu_ref[0].astype(BF))
    h = (hg * _sigmoid(hg) * hu * g).astype(BF)
    o_ref[...] += _dot(h, wd_ref[0].astype(BF))


def _moe(t, gate, x1, w_gate, w_up, w_down, layer, tm):
    n = t.shape[0]
    row = lambda i, e: (i, 0)
    ex = lambda i, e: (layer, e, 0, 0)
    est = 2 * tm * D_MODEL * (2 + 4 + 4) + 2 * 3 * 4 * D_MODEL * D_EXPERT + 6 * tm * D_EXPERT * 4
    return pl.pallas_call(
        _moe_kernel, grid=(n // tm, N_EXPERTS),
        in_specs=[pl.BlockSpec((tm, D_MODEL), row), pl.BlockSpec((tm, LANES), row), pl.BlockSpec((tm, D_MODEL), row),
                  pl.BlockSpec((None, 1, D_MODEL, D_EXPERT), ex), pl.BlockSpec((None, 1, D_MODEL, D_EXPERT), ex),
                  pl.BlockSpec((None, 1, D_EXPERT, D_MODEL), ex)],
        out_specs=pl.BlockSpec((tm, D_MODEL), row),
        out_shape=jax.ShapeDtypeStruct((n, D_MODEL), F32),
        compiler_params=pltpu.CompilerParams(dimension_semantics=("arbitrary", "arbitrary"),
                                             vmem_limit_bytes=_vmem_limit(est)),
        name="moe")(t, gate, x1, w_gate, w_up, w_down)


def _group_mean_matrix(width, groups):
    m = np.zeros((width, width), np.float32)
    for a, b in groups:
        m[a:b, a:b] = 1.0 / (b - a)
    return jnp.asarray(m, BF)


def _pad_heads(wm, n_heads, width, offset=0, stride=None):
    stride = stride or width
    k = wm.shape[0]
    wm = wm.reshape(k, n_heads, stride)[:, :, offset:offset + width]
    return jnp.pad(wm, ((0, 0), (0, 0), (0, HEAD_PAD - width))).reshape(k, n_heads * HEAD_PAD)


def _lane_row(parts):
    return jnp.concatenate([jnp.asarray(p, F32).reshape(-1) for p in parts])[None, :]


def _layer_weights(l, a):
    w = {}
    w_in = a['w_in'][l]
    cuts = np.cumsum([0, Q_RANK, KV_RANK, QK_ROPE, CONV_WIDTH, CONV_WIDTH, MEM_WIDTH])
    c_q, c_kv, k_r, u_v, u_g, q_m = [w_in[:, cuts[i]:cuts[i + 1]] for i in range(6)]
    k_r = jnp.pad(k_r, ((0, 0), (_ROPE0, LANES - _ROPE0 - QK_ROPE)))
    w['win'] = jnp.concatenate([c_q, c_kv, u_v, u_g, q_m, k_r], axis=1).astype(BF)
    w['n1g'] = a['norm1_g'][l][None, :]
    w['qag'] = a['q_a_norm_g'][l][None, :]
    w['kvg'] = a['kv_a_norm_g'][l][None, :]
    w['wuq'] = _pad_heads(a['w_uq'][l], N_HEADS, QK_NOPE + QK_ROPE).astype(BF)
    w_ukv = a['w_ukv'][l]
    w['wuk'] = _pad_heads(w_ukv, N_HEADS, QK_NOPE, 0, QK_NOPE + V_HEAD).astype(BF)
    w['wuv'] = _pad_heads(w_ukv, N_HEADS, V_HEAD, QK_NOPE, QK_NOPE + V_HEAD).astype(BF)
    zeros = lambda n: np.zeros((n,), np.float32)
    w['gq'] = _lane_row([a['qn_nope_g'][l], a['qn_rope_g'][l], zeros(LANES - 96)]) * (MLA_SCALE * LOG2E)
    w['gk'] = _lane_row([a['kn_nope_g'][l], zeros(LANES - QK_NOPE)])
    w['gkr'] = _lane_row([zeros(_ROPE0), a['kn_rope_g'][l], zeros(LANES - 96)])
    qg, kg = [], []
    for hd in range(N_HEADS):
        qg += [(hd * HEAD_PAD, hd * HEAD_PAD + QK_NOPE), (hd * HEAD_PAD + QK_NOPE, hd * HEAD_PAD + 96)]
        kg += [(hd * HEAD_PAD, hd * HEAD_PAD + QK_NOPE)]
    w['grp_q'] = _group_mean_matrix(N_HEADS * HEAD_PAD, qg)
    w['grp_k'] = _group_mean_matrix(N_HEADS * HEAD_PAD, kg)
    w['grp_m'] = _group_mean_matrix(MEM_WIDTH, [(i * MEM_HEAD_DIM, (i + 1) * MEM_HEAD_DIM) for i in range(MEM_HEADS)])
    w['mqg'] = jnp.tile(a['mem_qn_g'][l], MEM_HEADS)[None, :] * MEM_SCALE
    w['mem_kn_g'] = jnp.tile(a['mem_kn_g'][l], MEM_HEADS)[None, :]
    w['mem_norm_g'] = a['mem_norm_g'][l][None, :]
    w['w_mem_k'] = a['w_mem_k'][l].astype(BF)
    w['w_mem_v'] = a['w_mem_v'][l].astype(BF)
    w['conv_w'] = a['conv_w'][l]
    w['conv_b'] = a['conv_b'][l][None, :]
    w['conv_ln_g'] = a['conv_ln_g'][l][None, :]
    w['conv_ln_b'] = a['conv_ln_b'][l][None, :]
    w['w_out'] = a['w_out'][l].astype(BF)
    w['norm2_g'] = a['norm2_g'][l][None, :]
    w_router = jnp.pad(jnp.concatenate([a['w_router_expert'][l], a['w_router_group'][l]], axis=1),
                       ((0, 0), (0, LANES - N_EXPERTS - N_GROUPS)))
    w['w_router_hi'] = w_router.astype(BF)
    w['w_router_lo'] = (w_router - w['w_router_hi'].astype(F32)).astype(BF)
    w['b_router'] = jnp.pad(jnp.concatenate([a['b_router_expert'][l], a['b_router_group'][l]]),
                            (0, LANES - N_EXPERTS - N_GROUPS))[None, :]
    w_uk = w_ukv.reshape(KV_RANK, N_HEADS, QK_NOPE + V_HEAD)[:, :, :QK_NOPE]
    w['wuk_lat'] = jnp.transpose(w_uk, (0, 2, 1)).reshape(KV_RANK, QK_NOPE * N_HEADS).astype(BF)
    w['wuv_lat'] = w_ukv.reshape(KV_RANK, N_HEADS, QK_NOPE + V_HEAD)[:, :, QK_NOPE:].reshape(
        KV_RANK, MLA_WIDTH).astype(BF)
    w['wuk_heads'] = jnp.pad(jnp.transpose(w_uk, (1, 0, 2)), ((0, 0), (0, 0), (0, HEAD_PAD - QK_NOPE)))
    head_of_feat = np.arange(N_HEADS * QK_NOPE) // QK_NOPE
    head_of_qrow = np.arange(_QCOLS) % N_HEADS
    head_of_lane = np.arange(LANES) % N_HEADS
    w['grp_lat'] = jnp.asarray((head_of_qrow[:, None] == head_of_lane[None, :]) / QK_NOPE, BF)
    w['hm_lat'] = jnp.asarray(np.arange(N_HEADS)[:, None] == head_of_feat[None, :], F32)
    return w


def _rope_lane_tables(pos):
    inv_freq = 1.0 / (ROPE_THETA ** (jnp.arange(0, QK_ROPE, 2, dtype=F32) / QK_ROPE))
    ang = pos.astype(F32)[:, None] * inv_freq[None, :]
    cos, sin = jnp.cos(ang), jnp.sin(ang)
    n = pos.shape[0]
    one = jnp.ones((n, _ROPE0), F32)
    zero_hi = jnp.zeros((n, LANES - _ROPE0 - QK_ROPE), F32)
    c = jnp.concatenate([one, cos, cos, zero_hi], axis=1)
    s = jnp.concatenate([jnp.zeros((n, _ROPE0), F32), -sin, sin, zero_hi], axis=1)
    return c, s


def _pick_tile(n, pref):
    return pref if n % pref == 0 else n


def _block_diag_memory(k, v):
    m = k.shape[0]
    hm = (np.arange(MEM_WIDTH)[:, None] // MEM_HEAD_DIM) == (np.arange(MEM_HEADS * m)[None, :] // m)
    kx = jnp.where(hm, jnp.tile(k.T, (1, MEM_HEADS)), 0.0).astype(BF)
    vx = jnp.where(hm.T, jnp.tile(v, (MEM_HEADS, 1)), 0.0).astype(BF)
    return kx, vx


def kernel(x_prompt, x_sample, cache_ckv, cache_kpe, state_conv, cache_mem_k, cache_mem_v, page_table, mem_prompt, norm1_g, w_in, q_a_norm_g, w_uq, kv_a_norm_g, w_ukv, qn_nope_g, qn_rope_g, kn_nope_g, kn_rope_g, conv_w, conv_b, conv_ln_g, conv_ln_b, mem_norm_g, w_mem_k, w_mem_v, mem_qn_g, mem_kn_g, w_out, norm2_g, w_router_group, b_router_group, w_router_expert, b_router_expert, w_gate, w_up, w_down):
    params = dict(norm1_g=norm1_g, w_in=w_in, q_a_norm_g=q_a_norm_g, w_uq=w_uq, kv_a_norm_g=kv_a_norm_g,
                  w_ukv=w_ukv, qn_nope_g=qn_nope_g, qn_rope_g=qn_rope_g, kn_nope_g=kn_nope_g, kn_rope_g=kn_rope_g,
                  conv_w=conv_w, conv_b=conv_b, conv_ln_g=conv_ln_g, conv_ln_b=conv_ln_b, mem_norm_g=mem_norm_g,
                  w_mem_k=w_mem_k, w_mem_v=w_mem_v, mem_qn_g=mem_qn_g, mem_kn_g=mem_kn_g, w_out=w_out,
                  norm2_g=norm2_g, w_router_group=w_router_group, b_router_group=b_router_group,
                  w_router_expert=w_router_expert, b_router_expert=b_router_expert, w_gate=w_gate, w_up=w_up,
                  w_down=w_down)
    depth = w_in.shape[0]
    n_b, n_seq, _ = x_prompt.shape
    assert n_b == 1, "the prompt group is a single sequence"
    n_db, n_dec, _ = x_sample.shape
    n_pages = page_table.shape[1]
    past_len = n_pages * PAGE_SIZE
    n_mem = mem_prompt.shape[1]
    tp = n_b * n_seq
    ts = n_db * n_dec
    assert n_dec <= SUBLANES and n_dec * N_HEADS == _QCOLS

    cos_p, sin_p = _rope_lane_tables(jnp.arange(n_seq))
    cos_s, sin_s = _rope_lane_tables(jnp.tile(past_len + jnp.arange(n_dec), n_db))
    mem_head_mask = jnp.asarray(np.arange(MEM_HEADS)[:, None] == (np.arange(MEM_WIDTH)[None, :] // MEM_HEAD_DIM), F32)
    to_feature_major = lambda c: jnp.transpose(c, (0, 1, 3, 4, 2)).reshape(c.shape[:2] + (MEM_WIDTH, c.shape[2]))
    cache_mk = to_feature_major(cache_mem_k)
    cache_mv = to_feature_major(cache_mem_v)
    cache_kpe_t = jnp.swapaxes(cache_kpe, 2, 3)

    tm_p = _pick_tile(tp, 512)
    tq = _pick_tile(tp, 1024)
    tm_moe = _pick_tile(tp, 1024)
    npg = _PAGES_PER_STEP if n_pages % _PAGES_PER_STEP == 0 else n_pages
    nseq_mem = 8 if n_db % 8 == 0 else n_db

    xp = x_prompt.reshape(tp, D_MODEL)
    xs = x_sample.reshape(ts, D_MODEL)
    outs = {k: [] for k in ('ckv_p', 'kpe_p', 'conv_p', 'memk_p', 'memv_p', 'ckv_s', 'kpe_s', 'conv_s')}
    for l in range(depth):
        w = _layer_weights(l, params)

        mk, mv = _memkv(mem_prompt.reshape(n_mem, D_MODEL), w)
        q, k, v, ckv, kpe, glu, qm = _premix(xp, w, cos_p, sin_p, tm_p)
        a_mla = _flash_attention(q, k, v, tq)
        padded = jnp.concatenate([jnp.zeros((_CONV_HALO, CONV_WIDTH), F32), glu], axis=0)
        a_conv = _conv_rows(padded, w, tm_p)
        kx, vx = _block_diag_memory(mk, mv)
        a_mem = _memattn_rows(qm, kx, vx, tm_p)
        x1, t, gate = _outproj(xp, a_mla, a_conv, a_mem, w, tm_p)
        xp = _moe(t, gate, x1, w_gate, w_up, w_down, l, tm_moe)
        outs['ckv_p'].append(ckv.reshape(n_b, n_seq, KV_RANK))
        outs['kpe_p'].append(kpe.reshape(n_b, n_seq, QK_ROPE))
        outs['conv_p'].append(padded[-(CONV_K - 1):].reshape(n_b, CONV_K - 1, CONV_WIDTH))
        outs['memk_p'].append(mk.reshape(n_b, n_mem, MEM_HEADS, MEM_HEAD_DIM))
        outs['memv_p'].append(mv.reshape(n_b, n_mem, MEM_HEADS, MEM_HEAD_DIM))

        q, k, v, ckv, kpe, glu, qm = _premix(xs, w, cos_s, sin_s, _pick_tile(ts, 512))
        qlat = _absorb(q, w['gk'], w['wuk_heads'])
        per_seq_rows = lambda a: jnp.transpose(a.reshape(N_HEADS, a.shape[1], n_db, n_dec),
                                               (2, 3, 0, 1)).reshape(n_db, _QCOLS, a.shape[1])
        qa = per_seq_rows(qlat)
        qr = per_seq_rows(q[:, _ROPE0:_ROPE0 + QK_ROPE, :])
        new_ckv = jnp.pad(ckv.reshape(n_db, n_dec, KV_RANK), ((0, 0), (0, 2 * SUBLANES - n_dec), (0, 0)))
        new_kpe_t = jnp.pad(jnp.swapaxes(kpe.reshape(n_db, n_dec, QK_ROPE), 1, 2),
                            ((0, 0), (0, 0), (0, PAGE_SIZE - n_dec)))
        a_mla = _paged_attention(page_table, qa, qr, new_ckv, new_kpe_t,
                                 cache_ckv, cache_kpe_t, l, w, npg, n_dec).reshape(ts, MLA_WIDTH)
        padded = jnp.concatenate([state_conv[l], glu.reshape(n_db, n_dec, CONV_WIDTH)], axis=1)
        a_conv = jnp.transpose(_conv_steps(jnp.transpose(padded, (1, 0, 2)), w), (1, 0, 2)).reshape(ts, CONV_WIDTH)
        q8 = jnp.pad(qm.reshape(n_db, n_dec, MEM_WIDTH), ((0, 0), (0, SUBLANES - n_dec), (0, 0)))
        a_mem = _memattn_seqs(q8, cache_mk, cache_mv, l, mem_head_mask, nseq_mem)[:, :n_dec].reshape(ts, MEM_WIDTH)
        x1, t, gate = _outproj(xs, a_mla, a_conv, a_mem, w, _pick_tile(ts, 512))
        xs = _moe(t, gate, x1, w_gate, w_up, w_down, l, _pick_tile(ts, 512))
        outs['ckv_s'].append(ckv.reshape(n_db, n_dec, KV_RANK))
        outs['kpe_s'].append(kpe.reshape(n_db, n_dec, QK_ROPE))
        outs['conv_s'].append(padded[:, -(CONV_K - 1):])

    st = lambda key: jnp.stack(outs[key])
    return (xp.reshape(n_b, n_seq, D_MODEL), xs.reshape(n_db, n_dec, D_MODEL), st('ckv_p'), st('kpe_p'),
            st('conv_p'), st('memk_p'), st('memv_p'), st('ckv_s'), st('kpe_s'), st('conv_s'))
```

```python
import functools

import numpy as np
import jax
import jax.numpy as jnp
from jax import lax
from jax.experimental import pallas as pl
from jax.experimental.pallas import tpu as pltpu

D_MODEL = 1024
N_HEADS = 8
QK_NOPE = 64
QK_ROPE = 32
V_HEAD = 64
Q_RANK = 384
KV_RANK = 256
MLA_WIDTH = N_HEADS * V_HEAD
CONV_WIDTH = 256
MEM_WIDTH = 256
CONV_K = 31
MEM_HEADS = 4
MEM_HEAD_DIM = 64
N_GROUPS = 4
EXPERTS_PER_GROUP = 8
N_EXPERTS = 32
D_EXPERT = 256
PAGE_SIZE = 128
ROPE_THETA = 10000.0
EPS = 1e-6
MLA_SCALE = (QK_NOPE + QK_ROPE) ** -0.5
MEM_SCALE = MEM_HEAD_DIM ** -0.5

LANES = 128
SUBLANES = 8
VMEM_BYTES = 64 << 20
HEAD_PAD = LANES

_CQ0, _CKV0, _UV0, _UG0, _QM0, _KR0, _IN_PAD = 0, 384, 640, 896, 1152, 1408, 1536
_ROPE0 = QK_NOPE
_HALF = QK_ROPE // 2
NEG = -1e30
LOG2E = 1.4426950408889634
V_ROWS = 80

BF = jnp.bfloat16
F32 = jnp.float32


def _vmem_limit(nbytes):
    return int(min(max(2 * nbytes, 16 << 20), VMEM_BYTES - (8 << 20)))


def _dot(a, b):
    return jnp.dot(a, b, preferred_element_type=F32)


def _dot_nt(a, b):
    return lax.dot_general(a, b, (((1,), (1,)), ((), ())), preferred_element_type=F32)


def _rms(x, g):
    return x * lax.rsqrt(jnp.mean(x * x, axis=-1, keepdims=True) + EPS) * g


def _sigmoid(x):
    return 1.0 / (1.0 + jnp.exp(-x))


def _rope_padded(x, c, s):
    lane = lax.broadcasted_iota(jnp.int32, x.shape, 1)
    swapped = jnp.where(lane < _ROPE0 + _HALF, pltpu.roll(x, LANES - _HALF, 1), pltpu.roll(x, _HALF, 1))
    return x * c + swapped * s


def _premix_kernel(x_ref, n1g_ref, win_ref, qag_ref, wuq_ref, kvg_ref, wuk_ref, wuv_ref, gq_ref, gk_ref,
                   gkr_ref, grp_q_ref, grp_k_ref, grp_m_ref, mqg_ref, cos_ref, sin_ref,
                   q_out, k_out, v_out, ckv_out, kpe_out, glu_out, qm_out):
    x = x_ref[...]
    h = _rms(x, n1g_ref[...]).astype(BF)

    def seg(a, b):
        return _dot(h, win_ref[:, a:b])

    c = cos_ref[...]
    s = sin_ref[...]

    cq = _rms(seg(_CQ0, _CKV0), qag_ref[...]).astype(BF)
    q = _dot(cq, wuq_ref[...])
    q = q * lax.rsqrt(_dot((q * q).astype(BF), grp_q_ref[...]) + EPS)
    gq = gq_ref[...]
    for hd in range(N_HEADS):
        qh = q[:, hd * HEAD_PAD:(hd + 1) * HEAD_PAD] * gq
        q_out[hd] = _rope_padded(qh, c, s).T.astype(BF)

    ckv = _rms(seg(_CKV0, _UV0), kvg_ref[...])
    ckv_out[...] = ckv
    ckvb = ckv.astype(BF)
    kr = seg(_KR0, _IN_PAD)
    kr = kr * lax.rsqrt(jnp.sum(kr * kr, axis=-1, keepdims=True) * (1.0 / QK_ROPE) + EPS) * gkr_ref[...]
    kpe = _rope_padded(kr, c, s)
    kpe_out[...] = kpe[:, _ROPE0:_ROPE0 + QK_ROPE]

    kx = _dot(ckvb, wuk_ref[...])
    kx = kx * lax.rsqrt(_dot((kx * kx).astype(BF), grp_k_ref[...]) + EPS)
    vx = _dot(ckvb, wuv_ref[...])
    gk = gk_ref[...]
    ones_row = lax.broadcasted_iota(jnp.int32, (V_ROWS, x.shape[0]), 0) == V_HEAD
    for hd in range(N_HEADS):
        sl = slice(hd * HEAD_PAD, (hd + 1) * HEAD_PAD)
        k_out[hd] = (kx[:, sl] * gk + kpe).astype(BF)
        v_out[hd] = jnp.where(ones_row, 1.0, vx[:, sl].T[:V_ROWS]).astype(BF)

    glu_out[...] = seg(_UV0, _UG0) * _sigmoid(seg(_UG0, _QM0))
    qm = seg(_QM0, _KR0)
    qm = qm * lax.rsqrt(_dot((qm * qm).astype(BF), grp_m_ref[...]) + EPS) * mqg_ref[...]
    qm_out[...] = qm.astype(BF)


def _premix(x, w, cos_t, sin_t, tm):
    t = x.shape[0]
    nt = t // tm
    row = lambda i: (i, 0)
    full = lambda i: (0, 0)
    hm = lambda i: (0, i, 0)
    wspec = lambda a: pl.BlockSpec(a.shape, full)
    ins = [x, w['n1g'], w['win'], w['qag'], w['wuq'], w['kvg'], w['wuk'], w['wuv'], w['gq'], w['gk'], w['gkr'],
           w['grp_q'], w['grp_k'], w['grp_m'], w['mqg'], cos_t, sin_t]
    in_specs = [pl.BlockSpec((tm, D_MODEL), row)] + [wspec(a) for a in ins[1:15]] + \
               [pl.BlockSpec((tm, LANES), row), pl.BlockSpec((tm, LANES), row)]
    fm = lambda i: (0, 0, i)
    out_shape = (jax.ShapeDtypeStruct((N_HEADS, HEAD_PAD, t), BF), jax.ShapeDtypeStruct((N_HEADS, t, HEAD_PAD), BF),
                 jax.ShapeDtypeStruct((N_HEADS, V_ROWS, t), BF),
                 jax.ShapeDtypeStruct((t, KV_RANK), F32), jax.ShapeDtypeStruct((t, QK_ROPE), F32),
                 jax.ShapeDtypeStruct((t, CONV_WIDTH), F32), jax.ShapeDtypeStruct((t, MEM_WIDTH), BF))
    out_specs = (pl.BlockSpec((N_HEADS, HEAD_PAD, tm), fm), pl.BlockSpec((N_HEADS, tm, HEAD_PAD), hm),
                 pl.BlockSpec((N_HEADS, V_ROWS, tm), fm),
                 pl.BlockSpec((tm, KV_RANK), row), pl.BlockSpec((tm, QK_ROPE), row),
                 pl.BlockSpec((tm, CONV_WIDTH), row), pl.BlockSpec((tm, MEM_WIDTH), row))
    wbytes = sum(int(np.prod(a.shape)) * a.dtype.itemsize for a in ins[1:15])
    est = 2 * wbytes + tm * (2 * 4 * D_MODEL + 3 * 2 * 2 * N_HEADS * HEAD_PAD + 8 * 4 * N_HEADS * HEAD_PAD)
    return pl.pallas_call(
        _premix_kernel, grid=(nt,), in_specs=in_specs, out_specs=out_specs, out_shape=out_shape,
        compiler_params=pltpu.CompilerParams(dimension_semantics=("arbitrary",),
                                             vmem_limit_bytes=_vmem_limit(est)),
        name="premix")(*ins)


_EXP2_HEADROOM = 60.0


def _flash_kernel(qi_ref, kj_ref, qt_ref, k_ref, vt_ref, o_ref, m_scr, acc_scr):
    p = pl.program_id(0)
    i = qi_ref[p]
    j = kj_ref[p]

    def scores(hd, masked):
        st = _dot(k_ref[hd], qt_ref[hd])
        if masked:
            kk = lax.broadcasted_iota(jnp.int32, st.shape, 0)
            qq = lax.broadcasted_iota(jnp.int32, st.shape, 1)
            st = jnp.where(kk <= qq, st, NEG)
        return st

    def sweep(masked, first):
        def body(hd, carry):
            st = scores(hd, masked)
            if first:
                m = jnp.max(st, axis=0, keepdims=True)
                acc_scr[hd] = _dot(vt_ref[hd], jnp.exp2(st - m).astype(BF))
                m_scr[hd] = m
                return carry
            m_ref = m_scr[hd]
            pt = jnp.exp2(st - m_ref).astype(BF)
            gap = jnp.max(jnp.max(st, axis=0, keepdims=True) - m_ref)

            @pl.when(gap <= _EXP2_HEADROOM)
            def _():
                acc_scr[hd] += _dot(vt_ref[hd], pt)

            @pl.when(gap > _EXP2_HEADROOM)
            def _():
                st2 = scores(hd, masked)
                m_new = jnp.maximum(m_ref, jnp.max(st2, axis=0, keepdims=True))
                acc_scr[hd] = jnp.exp2(m_ref - m_new) * acc_scr[hd] + _dot(vt_ref[hd], jnp.exp2(st2 - m_new).astype(BF))
                m_scr[hd] = m_new
            return carry
        lax.fori_loop(0, N_HEADS, body, 0)

    @pl.when((j == 0) & (i > 0))
    def _():
        sweep(False, True)

    @pl.when((j > 0) & (j < i))
    def _():
        sweep(False, False)

    @pl.when((j == i) & (i == 0))
    def _():
        sweep(True, True)

    @pl.when((j == i) & (i > 0))
    def _():
        sweep(True, False)

    @pl.when(j == i)
    def _():
        for hp in range(N_HEADS // 2):
            halves = []
            for hd in (2 * hp, 2 * hp + 1):
                acc = acc_scr[hd]
                halves.append((acc[:V_HEAD] / acc[V_HEAD:V_HEAD + 1]).T)
            o_ref[:, hp * LANES:(hp + 1) * LANES] = jnp.concatenate(halves, axis=1).astype(o_ref.dtype)


def _flash_attention(qt, k, vt, tq):
    t = k.shape[1]
    nq = t // tq
    qi = np.concatenate([np.full(i + 1, i, np.int32) for i in range(nq)])
    kj = np.concatenate([np.arange(i + 1, dtype=np.int32) for i in range(nq)])
    grid_spec = pltpu.PrefetchScalarGridSpec(
        num_scalar_prefetch=2, grid=(len(qi),),
        in_specs=[pl.BlockSpec((N_HEADS, HEAD_PAD, tq), lambda p, qi, kj: (0, 0, qi[p])),
                  pl.BlockSpec((N_HEADS, tq, HEAD_PAD), lambda p, qi, kj: (0, kj[p], 0)),
                  pl.BlockSpec((N_HEADS, V_ROWS, tq), lambda p, qi, kj: (0, 0, kj[p]))],
        out_specs=pl.BlockSpec((tq, MLA_WIDTH), lambda p, qi, kj: (qi[p], 0)),
        scratch_shapes=[pltpu.VMEM((N_HEADS, 1, tq), F32), pltpu.VMEM((N_HEADS, V_ROWS, tq), F32)])
    nblk = N_HEADS * tq * HEAD_PAD
    est = 3 * 2 * 2 * nblk + 4 * nblk + 2 * 2 * tq * MLA_WIDTH + 3 * 4 * tq * tq
    return pl.pallas_call(
        _flash_kernel, grid_spec=grid_spec,
        out_shape=jax.ShapeDtypeStruct((t, MLA_WIDTH), BF),
        compiler_params=pltpu.CompilerParams(dimension_semantics=("arbitrary",),
                                             vmem_limit_bytes=_vmem_limit(est)),
        name="flash")(jnp.asarray(qi), jnp.asarray(kj), qt, k, vt)


_CONV_HALO = 32
_CONV_LEAD = _CONV_HALO - (CONV_K - 1)

def _ln_swish(y, g, b):
    mu = jnp.mean(y, axis=-1, keepdims=True)
    d = y - mu
    y = d * lax.rsqrt(jnp.mean(d * d, axis=-1, keepdims=True) + EPS) * g + b
    return y * _sigmoid(y)


def _conv_rows_kernel(main_ref, halo_ref, w_ref, b_ref, g_ref, lb_ref, o_ref, buf, *, tm):
    buf[0:tm, :] = main_ref[...]
    buf[tm:tm + _CONV_HALO, :] = halo_ref[...]
    acc = jnp.zeros((tm, CONV_WIDTH), F32)
    for j in range(CONV_K):
        acc = acc + buf[_CONV_LEAD + j:_CONV_LEAD + j + tm, :] * w_ref[j:j + 1, :]
    o_ref[...] = _ln_swish(acc + b_ref[...], g_ref[...], lb_ref[...]).astype(o_ref.dtype)


def _conv_rows(padded, w, tm):
    t = padded.shape[0] - _CONV_HALO
    full = lambda i: (0, 0)
    hb = tm // _CONV_HALO
    return pl.pallas_call(
        functools.partial(_conv_rows_kernel, tm=tm), grid=(t // tm,),
        in_specs=[pl.BlockSpec((tm, CONV_WIDTH), lambda i: (i, 0)),
                  pl.BlockSpec((_CONV_HALO, CONV_WIDTH), lambda i: ((i + 1) * hb, 0)),
                  pl.BlockSpec((CONV_K, CONV_WIDTH), full)] + [pl.BlockSpec((1, CONV_WIDTH), full)] * 3,
        out_specs=pl.BlockSpec((tm, CONV_WIDTH), lambda i: (i, 0)),
        out_shape=jax.ShapeDtypeStruct((t, CONV_WIDTH), BF),
        scratch_shapes=[pltpu.VMEM((tm + _CONV_HALO, CONV_WIDTH), F32)],
        compiler_params=pltpu.CompilerParams(dimension_semantics=("arbitrary",)),
        name="conv_rows")(padded, padded, w['conv_w'], w['conv_b'], w['conv_ln_g'], w['conv_ln_b'])


def _conv_steps_kernel(p_ref, w_ref, b_ref, g_ref, lb_ref, o_ref, *, n_new):
    for t in range(n_new):
        acc = jnp.zeros(o_ref.shape[1:], F32)
        for j in range(CONV_K):
            acc = acc + p_ref[t + j] * w_ref[j:j + 1, :]
        o_ref[t] = _ln_swish(acc + b_ref[...], g_ref[...], lb_ref[...]).astype(o_ref.dtype)


def _conv_steps(padded_t, w):
    n_new = padded_t.shape[0] - (CONV_K - 1)
    nb = padded_t.shape[1]
    full2 = lambda i: (0, 0)
    full3 = lambda i: (0, 0, 0)
    return pl.pallas_call(
        functools.partial(_conv_steps_kernel, n_new=n_new), grid=(1,),
        in_specs=[pl.BlockSpec(padded_t.shape, full3), pl.BlockSpec((CONV_K, CONV_WIDTH), full2)] +
                 [pl.BlockSpec((1, CONV_WIDTH), full2)] * 3,
        out_specs=pl.BlockSpec((n_new, nb, CONV_WIDTH), full3),
        out_shape=jax.ShapeDtypeStruct((n_new, nb, CONV_WIDTH), BF),
        name="conv_steps")(padded_t, w['conv_w'], w['conv_b'], w['conv_ln_g'], w['conv_ln_b'])


def _memkv_kernel(mem_ref, g_ref, wk_ref, wv_ref, grp_ref, kg_ref, k_out, v_out):
    m = _rms(mem_ref[...], g_ref[...]).astype(BF)
    k = _dot(m, wk_ref[...])
    k_out[...] = k * lax.rsqrt(_dot((k * k).astype(BF), grp_ref[...]) + EPS) * kg_ref[...]
    v_out[...] = _dot(m, wv_ref[...])


def _memkv(mem, w):
    n = mem.shape[0]
    full = lambda i: (0, 0)
    ins = [mem, w['mem_norm_g'], w['w_mem_k'], w['w_mem_v'], w['grp_m'], w['mem_kn_g']]
    o = jax.ShapeDtypeStruct((n, MEM_WIDTH), F32)
    return pl.pallas_call(
        _memkv_kernel, grid=(1,), in_specs=[pl.BlockSpec(a.shape, full) for a in ins],
        out_specs=(pl.BlockSpec((n, MEM_WIDTH), full),) * 2, out_shape=(o, o),
        name="memkv")(*ins)


def _memattn_rows_kernel(q_ref, kx_ref, vx_ref, o_ref, *, n_mem):
    s = _dot(q_ref[...], kx_ref[...])
    parts = []
    for hd in range(MEM_HEADS):
        sh = s[:, hd * n_mem:(hd + 1) * n_mem]
        e = jnp.exp(sh - jnp.max(sh, axis=1, keepdims=True))
        parts.append((e / jnp.sum(e, axis=1, keepdims=True)).astype(BF))
    o_ref[...] = _dot(jnp.concatenate(parts, axis=1), vx_ref[...]).astype(o_ref.dtype)


def _memattn_rows(qm, kx, vx, tm):
    t = qm.shape[0]
    n_mem = vx.shape[0] // MEM_HEADS
    full = lambda i: (0, 0)
    return pl.pallas_call(
        functools.partial(_memattn_rows_kernel, n_mem=n_mem), grid=(t // tm,),
        in_specs=[pl.BlockSpec((tm, MEM_WIDTH), lambda i: (i, 0)), pl.BlockSpec(kx.shape, full),
                  pl.BlockSpec(vx.shape, full)],
        out_specs=pl.BlockSpec((tm, MEM_WIDTH), lambda i: (i, 0)),
        out_shape=jax.ShapeDtypeStruct((t, MEM_WIDTH), BF),
        compiler_params=pltpu.CompilerParams(dimension_semantics=("arbitrary",)),
        name="memattn_rows")(qm, kx, vx)


def _memattn_seqs_kernel(q_ref, k_ref, v_ref, hm_ref, o_ref, *, nseq):
    hm = hm_ref[...]
    for b in range(nseq):
        q8 = q_ref[b].astype(F32)
        qe = jnp.concatenate([q8 * hm[hd:hd + 1, :] for hd in range(MEM_HEADS)], axis=0).astype(BF)
        s = _dot(qe, k_ref[0, b].astype(BF))
        e = jnp.exp(s - jnp.max(s, axis=1, keepdims=True))
        pr = (e / jnp.sum(e, axis=1, keepdims=True)).astype(BF)
        o = _dot_nt(pr, v_ref[0, b].astype(BF))
        acc = o[0:SUBLANES] * hm[0:1, :]
        for hd in range(1, MEM_HEADS):
            acc = acc + o[hd * SUBLANES:(hd + 1) * SUBLANES] * hm[hd:hd + 1, :]
        o_ref[b] = acc.astype(o_ref.dtype)


def _memattn_seqs(q8, cache_k, cache_v, layer, head_mask, nseq):
    nb = q8.shape[0]
    m = cache_k.shape[3]
    return pl.pallas_call(
        functools.partial(_memattn_seqs_kernel, nseq=nseq), grid=(nb // nseq,),
        in_specs=[pl.BlockSpec((nseq, SUBLANES, MEM_WIDTH), lambda i: (i, 0, 0)),
                  pl.BlockSpec((1, nseq, MEM_WIDTH, m), lambda i: (layer, i, 0, 0)),
                  pl.BlockSpec((1, nseq, MEM_WIDTH, m), lambda i: (layer, i, 0, 0)),
                  pl.BlockSpec(head_mask.shape, lambda i: (0, 0))],
        out_specs=pl.BlockSpec((nseq, SUBLANES, MEM_WIDTH), lambda i: (i, 0, 0)),
        out_shape=jax.ShapeDtypeStruct((nb, SUBLANES, MEM_WIDTH), BF),
        compiler_params=pltpu.CompilerParams(dimension_semantics=("arbitrary",)),
        name="memattn_seqs")(q8, cache_k, cache_v, head_mask)


_QCOLS = 32
_PAGES_PER_STEP = 32


def _absorb_kernel(qt_ref, gk_ref, wuk_ref, o_ref):
    for hd in range(N_HEADS):
        wg = (wuk_ref[hd] * gk_ref[...]).astype(BF)
        o_ref[hd] = _dot(wg, qt_ref[hd]).astype(o_ref.dtype)


def _absorb(qt, gk, wuk):
    t = qt.shape[2]
    f3 = lambda i: (0, 0, 0)
    return pl.pallas_call(
        _absorb_kernel, grid=(1,),
        in_specs=[pl.BlockSpec(qt.shape, f3), pl.BlockSpec(gk.shape, lambda i: (0, 0)), pl.BlockSpec(wuk.shape, f3)],
        out_specs=pl.BlockSpec((N_HEADS, KV_RANK, t), f3),
        out_shape=jax.ShapeDtypeStruct((N_HEADS, KV_RANK, t), BF),
        name="absorb")(qt, gk, wuk)


def _paged_kernel(pt_ref, qa_ref, qr_ref, nckv_ref, nkpe_ref, ckv_hbm, kpe_hbm, wuk_ref, grp_ref, wuv_ref, hm_ref,
                  o_ref, m_scr, l_scr, acc_scr, ckv_scr, kpe_scr, ckv_buf, kpe_buf, sems, *, npg, n_new, layer):
    c = pl.program_id(1)
    nchunk = pl.num_programs(1)
    step = pl.program_id(0) * nchunk + c
    nsteps = pl.num_programs(0) * nchunk
    slot = step % 2

    def page_copies(s, slot_):
        for r in range(npg):
            page = pt_ref[s * npg + r]
            yield pltpu.make_async_copy(ckv_hbm.at[layer, page], ckv_buf.at[slot_, r], sems.at[0, slot_])
            yield pltpu.make_async_copy(kpe_hbm.at[layer, page], kpe_buf.at[slot_, r], sems.at[1, slot_])

    @pl.when(step == 0)
    def _():
        for cp in page_copies(0, 0):
            cp.start()

    @pl.when(step + 1 < nsteps)
    def _():
        for cp in page_copies(step + 1, 1 - slot):
            cp.start()

    for cp in page_copies(step, slot):
        cp.wait()

    qa = qa_ref[0]
    qr = qr_ref[0]

    def key_energy(ckvb):
        kn = _dot(ckvb, wuk_ref[...])
        k2 = kn * kn
        return ((k2[:, 0:LANES] + k2[:, LANES:2 * LANES]) + (k2[:, 2 * LANES:3 * LANES] + k2[:, 3 * LANES:])).astype(BF)

    def attend(ckvb, kpet, causal):
        r = lax.rsqrt(_dot_nt(grp_ref[...], key_energy(ckvb)) + EPS)
        st = _dot_nt(qa, ckvb) * r + _dot(qr, kpet)
        if causal:
            row = lax.broadcasted_iota(jnp.int32, st.shape, 0)
            col = lax.broadcasted_iota(jnp.int32, st.shape, 1)
            st = jnp.where(col <= row // N_HEADS, st, NEG)
        m = jnp.max(st, axis=1, keepdims=True)
        pr = jnp.exp2(st - m)
        pb = pr.astype(BF)
        half = ckvb.shape[0] // 2
        pv = _dot(pb[:, :half], ckvb[:half]) + _dot(pb[:, half:], ckvb[half:])
        return m, jnp.sum(pr, axis=1, keepdims=True), pv

    def merge(parts):
        m_all = functools.reduce(jnp.maximum, [p[0] for p in parts])
        scales = [jnp.exp2(p[0] - m_all) for p in parts]
        m_scr[...] = m_all
        l_scr[...] = sum(s * p[1] for s, p in zip(scales, parts))
        acc_scr[...] = sum(s * p[2] for s, p in zip(scales, parts))

    @pl.when(c == 0)
    def _():
        ckv_scr[0:PAGE_SIZE, :] = jnp.zeros((PAGE_SIZE, KV_RANK), BF)
        ckv_scr[0:2 * SUBLANES, :] = nckv_ref[0].astype(BF)
        merge([attend(ckv_scr[0:PAGE_SIZE, :], nkpe_ref[0].astype(BF), True)])

    for r in range(npg):
        ckv_scr[r * PAGE_SIZE:(r + 1) * PAGE_SIZE, :] = ckv_buf[slot, r].astype(BF)
        kpe_scr[:, r * PAGE_SIZE:(r + 1) * PAGE_SIZE] = kpe_buf[slot, r].astype(BF)
    merge([(m_scr[...], l_scr[...], acc_scr[...]), attend(ckv_scr[...], kpe_scr[...], False)])

    @pl.when(c == pl.num_programs(1) - 1)
    def _():
        lat = (acc_scr[...] / l_scr[...]).astype(BF)
        full = _dot(lat, wuv_ref[...])
        hm = hm_ref[...]
        rows = [jnp.sum(full[qi * N_HEADS:(qi + 1) * N_HEADS] * hm, axis=0, keepdims=True)
                for qi in range(n_new)]
        o_ref[0] = jnp.concatenate(rows, axis=0).astype(o_ref.dtype)


def _paged_attention(page_table, qa, qr, nckv, nkpe, cache_ckv, cache_kpe, layer, w, npg, n_new):
    nb, n_pages = page_table.shape
    nchunk = n_pages // npg
    tc = npg * PAGE_SIZE

    per_seq = lambda shp: pl.BlockSpec((1,) + shp, lambda b, c, pt: (b, 0, 0))
    const = lambda a: pl.BlockSpec(a.shape, lambda b, c, pt: (0, 0))
    hbm = pl.BlockSpec(memory_space=pl.ANY)
    consts = [w['wuk_lat'], w['grp_lat'], w['wuv_lat'], w['hm_lat']]
    in_specs = ([per_seq((_QCOLS, KV_RANK)), per_seq((_QCOLS, QK_ROPE)), per_seq((2 * SUBLANES, KV_RANK)),
                 per_seq((QK_ROPE, PAGE_SIZE)), hbm, hbm] + [const(a) for a in consts])
    grid_spec = pltpu.PrefetchScalarGridSpec(
        num_scalar_prefetch=1, grid=(nb, nchunk), in_specs=in_specs,
        out_specs=pl.BlockSpec((1, n_new, MLA_WIDTH), lambda b, c, pt: (b, 0, 0)),
        scratch_shapes=[pltpu.VMEM((_QCOLS, 1), F32), pltpu.VMEM((_QCOLS, 1), F32),
                        pltpu.VMEM((_QCOLS, KV_RANK), F32),
                        pltpu.VMEM((tc, KV_RANK), BF), pltpu.VMEM((QK_ROPE, tc), BF),
                        pltpu.VMEM((2, npg, PAGE_SIZE, KV_RANK), F32), pltpu.VMEM((2, npg, QK_ROPE, PAGE_SIZE), F32),
                        pltpu.SemaphoreType.DMA((2, 2))])
    est = 2 * npg * PAGE_SIZE * (KV_RANK + QK_ROPE) * 4 + tc * (KV_RANK + LANES) * 2 + tc * (512 + 4 * LANES) * 4
    return pl.pallas_call(
        functools.partial(_paged_kernel, npg=npg, n_new=n_new, layer=layer), grid_spec=grid_spec,
        out_shape=jax.ShapeDtypeStruct((nb, n_new, MLA_WIDTH), BF),
        compiler_params=pltpu.CompilerParams(dimension_semantics=("arbitrary", "arbitrary"),
                                             vmem_limit_bytes=_vmem_limit(est)),
        name="paged")(page_table.reshape(-1), qa, qr, nckv, nkpe, cache_ckv, cache_kpe, *consts)


def _route(logits):
    lane_i = lax.broadcasted_iota(jnp.int32, logits.shape, 1)
    lane = lane_i.astype(F32)
    big = jnp.float32(1 << 20)
    is_g = (lane_i >= N_EXPERTS) & (lane_i < N_EXPERTS + N_GROUPS)
    gl = jnp.where(is_g, logits, NEG)
    gmax = jnp.max(gl, axis=1, keepdims=True)
    gsum = jnp.sum(jnp.where(is_g, jnp.exp(gl - gmax), 0.0), axis=1, keepdims=True)
    p_grp = 1.0 / gsum
    grp = jnp.min(jnp.where(is_g & (gl == gmax), lane, big), axis=1, keepdims=True) - N_EXPERTS
    sel = (lane_i < N_EXPERTS) & ((lane_i // EXPERTS_PER_GROUP).astype(F32) == grp)
    el = jnp.where(sel, logits, NEG)
    emax = jnp.max(el, axis=1, keepdims=True)
    ee = jnp.where(sel, jnp.exp(el - emax), 0.0)
    pe = ee / jnp.sum(ee, axis=1, keepdims=True)
    p1 = jnp.max(pe, axis=1, keepdims=True)
    i1 = jnp.min(jnp.where(sel & (pe == p1), lane, big), axis=1, keepdims=True)
    pe2 = jnp.where(sel & (lane != i1), pe, -1.0)
    p2 = jnp.max(pe2, axis=1, keepdims=True)
    i2 = jnp.min(jnp.where(pe2 == p2, lane, big), axis=1, keepdims=True)
    norm = p_grp / (p1 + p2)
    return lane, i1, i2, p1 * norm, p2 * norm


def _mix_norm_route(x_ref, a_ref, c_ref, m_ref, wo_ref, g2_ref, wrh_ref, wrl_ref, br_ref):
    y = _dot(a_ref[...], wo_ref[0:MLA_WIDTH, :])
    y = y + _dot(c_ref[...], wo_ref[MLA_WIDTH:MLA_WIDTH + CONV_WIDTH, :])
    y = y + _dot(m_ref[...], wo_ref[MLA_WIDTH + CONV_WIDTH:, :])
    x1 = x_ref[...] + y
    t = _rms(x1, g2_ref[...])
    th = t.astype(BF)
    tl = (t - th.astype(F32)).astype(BF)
    logits = _dot(th, wrh_ref[...]) + (_dot(tl, wrh_ref[...]) + _dot(th, wrl_ref[...])) + br_ref[...]
    return x1, th, _route(logits)


def _outproj_kernel(*refs):
    x1_out, t_out, gate_out = refs[9:]
    x1, th, (lane, i1, i2, g1, g2) = _mix_norm_route(*refs[:9])
    x1_out[...] = x1
    t_out[...] = th
    gate_out[...] = jnp.where(lane == i1, g1, jnp.where(lane == i2, g2, 0.0))


_META_E1, _META_E2, _META_G1, _META_G2, _META_R1, _META_R2 = range(6)


def _outproj_sort_kernel(*refs):
    tri_ref, x1_out, tpk_out, meta_out, cnt_out, cnt_scr = refs[9:]
    x1, th, (lane, i1, i2, g1, g2) = _mix_norm_route(*refs[:9])
    x1_out[...] = x1
    tf = th.astype(F32)
    half = D_MODEL // 2
    lo = lax.shift_right_logical(pltpu.bitcast(tf[:, :half], jnp.uint32), jnp.uint32(16))
    hi = pltpu.bitcast(tf[:, half:], jnp.uint32) & jnp.uint32(0xFFFF0000)
    tpk_out[...] = lo | hi

    @pl.when(pl.program_id(0) == 0)
    def _():
        cnt_scr[...] = jnp.zeros(cnt_scr.shape, F32)

    hit = jnp.where((lane == i1) | (lane == i2), 1.0, 0.0)
    before = _dot(tri_ref[...], hit.astype(BF)) + cnt_scr[0:1, :]
    r1 = jnp.sum(jnp.where(lane == i1, before, 0.0), axis=1, keepdims=True)
    r2 = jnp.sum(jnp.where(lane == i2, before, 0.0), axis=1, keepdims=True)
    counts = cnt_scr[0:1, :] + jnp.sum(hit, axis=0, keepdims=True)
    cnt_scr[...] = jnp.broadcast_to(counts, cnt_scr.shape)
    cnt_out[...] = jnp.broadcast_to(counts, cnt_out.shape)
    meta = jnp.zeros(lane.shape, F32)
    for idx, val in ((_META_E1, i1), (_META_E2, i2), (_META_G1, g1), (_META_G2, g2), (_META_R1, r1), (_META_R2, r2)):
        meta = jnp.where(lane == idx, val, meta)
    meta_out[...] = meta


def _outproj(x, a_mla, a_conv, a_mem, w, tm, sort):
    t = x.shape[0]
    row = lambda i: (i, 0)
    full = lambda i: (0, 0)
    ins = [x, a_mla, a_conv, a_mem, w['w_out'], w['norm2_g'], w['w_router_hi'], w['w_router_lo'], w['b_router']]
    in_specs = [pl.BlockSpec((tm, D_MODEL), row), pl.BlockSpec((tm, MLA_WIDTH), row),
                pl.BlockSpec((tm, CONV_WIDTH), row), pl.BlockSpec((tm, MEM_WIDTH), row)] + \
               [pl.BlockSpec(a.shape, full) for a in ins[4:]]
    est = 2 * tm * D_MODEL * (4 + 4 + 2 + 2) + 2 * 2 * D_MODEL * D_MODEL + 4 * tm * D_MODEL * 4
    x1_shape, x1_spec = jax.ShapeDtypeStruct((t, D_MODEL), F32), pl.BlockSpec((tm, D_MODEL), row)
    if sort:
        tri = jnp.asarray(np.tril(np.ones((tm, tm), np.float32), -1), BF)
        ins.append(tri)
        in_specs.append(pl.BlockSpec((tm, tm), full))
        out_shape = (x1_shape, jax.ShapeDtypeStruct((t, D_MODEL // 2), jnp.uint32),
                     jax.ShapeDtypeStruct((t, LANES), F32), jax.ShapeDtypeStruct((SUBLANES, LANES), F32))
        out_specs = (x1_spec, pl.BlockSpec((tm, D_MODEL // 2), row), pl.BlockSpec((tm, LANES), row),
                     pl.BlockSpec((SUBLANES, LANES), full))
        body, scratch, name = _outproj_sort_kernel, [pltpu.VMEM((SUBLANES, LANES), F32)], "outproj_sort"
    else:
        out_shape = (x1_shape, jax.ShapeDtypeStruct((t, D_MODEL), BF), jax.ShapeDtypeStruct((t, LANES), F32))
        out_specs = (x1_spec, pl.BlockSpec((tm, D_MODEL), row), pl.BlockSpec((tm, LANES), row))
        body, scratch, name = _outproj_kernel, [], "outproj"
    return pl.pallas_call(
        body, grid=(t // tm,), in_specs=in_specs, out_specs=out_specs, out_shape=out_shape, scratch_shapes=scratch,
        compiler_params=pltpu.CompilerParams(dimension_semantics=("arbitrary",),
                                             vmem_limit_bytes=_vmem_limit(est)),
        name=name)(*ins)


def _moe_kernel(t_ref, gate_ref, x1_ref, wg_ref, wu_ref, wd_ref, o_ref):
    e = pl.program_id(1)

    @pl.when(e == 0)
    def _():
        o_ref[...] = x1_ref[...]

    gate = gate_ref[...]
    lane = lax.broadcasted_iota(jnp.int32, gate.shape, 1)
    g = jnp.sum(jnp.where(lane == e, gate, 0.0), axis=1, keepdims=True)
    t = t_ref[...]
    hg = _dot(t, wg_ref[0].astype(BF))
    hu = _dot(t, wu_ref[0].astype(BF))
    h = (hg * _sigmoid(hg) * hu * g).astype(BF)
    o_ref[...] += _dot(h, wd_ref[0].astype(BF))


def _moe(t, gate, x1, w_gate, w_up, w_down, layer, tm):
    n = t.shape[0]
    row = lambda i, e: (i, 0)
    ex = lambda i, e: (layer, e, 0, 0)
    est = 2 * tm * D_MODEL * (2 + 4 + 4) + 2 * 3 * 4 * D_MODEL * D_EXPERT + 6 * tm * D_EXPERT * 4
    return pl.pallas_call(
        _moe_kernel, grid=(n // tm, N_EXPERTS),
        in_specs=[pl.BlockSpec((tm, D_MODEL), row), pl.BlockSpec((tm, LANES), row), pl.BlockSpec((tm, D_MODEL), row),
                  pl.BlockSpec((None, 1, D_MODEL, D_EXPERT), ex), pl.BlockSpec((None, 1, D_MODEL, D_EXPERT), ex),
                  pl.BlockSpec((None, 1, D_EXPERT, D_MODEL), ex)],
        out_specs=pl.BlockSpec((tm, D_MODEL), row),
        out_shape=jax.ShapeDtypeStruct((n, D_MODEL), F32),
        compiler_params=pltpu.CompilerParams(dimension_semantics=("arbitrary", "arbitrary"),
                                             vmem_limit_bytes=_vmem_limit(est)),
        name="moe")(t, gate, x1, w_gate, w_up, w_down)


_MOE_TILE = 256
_SCATTER_ROWS = 512
_COMBINE_ROWS = 256


def _scatter_kernel(p1_ref, p2_ref, src_ref, init_hbm, dst_hbm, sem, *, rows):
    del init_hbm
    base = pl.program_id(0) * rows

    def copies():
        for r in range(rows):
            yield pltpu.make_async_copy(src_ref.at[pl.ds(r, 1)], dst_hbm.at[pl.ds(p1_ref[base + r], 1)], sem.at[0])
            yield pltpu.make_async_copy(src_ref.at[pl.ds(r, 1)], dst_hbm.at[pl.ds(p2_ref[base + r], 1)], sem.at[0])

    for cp in copies():
        cp.start()
    for cp in copies():
        cp.wait()


def _scatter_rows(pos1, pos2, src, n_rows):
    t, width = src.shape
    rows = _SCATTER_ROWS if t % _SCATTER_ROWS == 0 else t
    hbm = pl.BlockSpec(memory_space=pl.ANY)
    grid_spec = pltpu.PrefetchScalarGridSpec(
        num_scalar_prefetch=2, grid=(t // rows,),
        in_specs=[pl.BlockSpec((rows, width), lambda i, p1, p2: (i, 0)), hbm], out_specs=hbm,
        scratch_shapes=[pltpu.SemaphoreType.DMA((1,))])
    return pl.pallas_call(
        functools.partial(_scatter_kernel, rows=rows), grid_spec=grid_spec,
        out_shape=jax.ShapeDtypeStruct((n_rows, width), src.dtype),
        input_output_aliases={3: 0},
        compiler_params=pltpu.CompilerParams(dimension_semantics=("arbitrary",)),
        name="moe_scatter")(pos1, pos2, src, jnp.zeros((n_rows, width), src.dtype))


def _expert_tiles_kernel(te_ref, nu_ref, x_ref, wg_ref, wu_ref, wd_ref, o_ref):
    i = pl.program_id(0)

    @pl.when(i < nu_ref[0])
    def _():
        u = x_ref[...]
        lo = pltpu.bitcast(lax.shift_left(u, jnp.uint32(16)), F32)
        hi = pltpu.bitcast(u & jnp.uint32(0xFFFF0000), F32)
        x = jnp.concatenate([lo, hi], axis=1).astype(BF)
        hg = _dot(x, wg_ref[0].astype(BF))
        hu = _dot(x, wu_ref[0].astype(BF))
        o_ref[...] = _dot((hg * _sigmoid(hg) * hu).astype(BF), wd_ref[0].astype(BF))

    @pl.when(i >= nu_ref[0])
    def _():
        o_ref[...] = jnp.zeros(o_ref.shape, F32)


def _expert_tiles(tile_expert, n_used, xs, w_gate, w_up, w_down, layer):
    s = xs.shape[0]
    ex = lambda i, te, nu: (layer, te[i], 0, 0)
    grid_spec = pltpu.PrefetchScalarGridSpec(
        num_scalar_prefetch=2, grid=(s // _MOE_TILE,),
        in_specs=[pl.BlockSpec((_MOE_TILE, D_MODEL // 2), lambda i, te, nu: (i, 0)),
                  pl.BlockSpec((None, 1, D_MODEL, D_EXPERT), ex), pl.BlockSpec((None, 1, D_MODEL, D_EXPERT), ex),
                  pl.BlockSpec((None, 1, D_EXPERT, D_MODEL), ex)],
        out_specs=pl.BlockSpec((_MOE_TILE, D_MODEL), lambda i, te, nu: (i, 0)))
    est = 2 * 3 * 4 * D_MODEL * D_EXPERT + 2 * _MOE_TILE * D_MODEL * 6 + 8 * _MOE_TILE * D_MODEL * 4
    return pl.pallas_call(
        _expert_tiles_kernel, grid_spec=grid_spec, out_shape=jax.ShapeDtypeStruct((s, D_MODEL), F32),
        compiler_params=pltpu.CompilerParams(dimension_semantics=("arbitrary",),
                                             vmem_limit_bytes=_vmem_limit(est)),
        name="moe_experts")(tile_expert, n_used, xs, w_gate, w_up, w_down)


def _combine_kernel(p1_ref, p2_ref, x1_ref, meta_ref, y_hbm, o_ref, buf, sems, *, rows):
    step = pl.program_id(0)
    slot = step % 2

    def copies(s, slot_):
        for r in range(rows):
            tok = s * rows + r
            yield pltpu.make_async_copy(y_hbm.at[pl.ds(p1_ref[tok], 1)], buf.at[slot_, 0, pl.ds(r, 1)], sems.at[slot_])
            yield pltpu.make_async_copy(y_hbm.at[pl.ds(p2_ref[tok], 1)], buf.at[slot_, 1, pl.ds(r, 1)], sems.at[slot_])

    @pl.when(step == 0)
    def _():
        for cp in copies(0, 0):
            cp.start()

    @pl.when(step + 1 < pl.num_programs(0))
    def _():
        for cp in copies(step + 1, 1 - slot):
            cp.start()

    for cp in copies(step, slot):
        cp.wait()
    meta = meta_ref[...]
    g1 = meta[:, _META_G1:_META_G1 + 1]
    g2 = meta[:, _META_G2:_META_G2 + 1]
    o_ref[...] = x1_ref[...] + g1 * buf[slot, 0] + g2 * buf[slot, 1]


def _combine_rows(pos1, pos2, x1, meta, y):
    t = x1.shape[0]
    rows = _COMBINE_ROWS if t % _COMBINE_ROWS == 0 else t
    row = lambda i, p1, p2: (i, 0)
    grid_spec = pltpu.PrefetchScalarGridSpec(
        num_scalar_prefetch=2, grid=(t // rows,),
        in_specs=[pl.BlockSpec((rows, D_MODEL), row), pl.BlockSpec((rows, LANES), row),
                  pl.BlockSpec(memory_space=pl.ANY)],
        out_specs=pl.BlockSpec((rows, D_MODEL), row),
        scratch_shapes=[pltpu.VMEM((2, 2, rows, D_MODEL), F32), pltpu.SemaphoreType.DMA((2,))])
    return pl.pallas_call(
        functools.partial(_combine_kernel, rows=rows), grid_spec=grid_spec,
        out_shape=jax.ShapeDtypeStruct((t, D_MODEL), F32),
        compiler_params=pltpu.CompilerParams(dimension_semantics=("arbitrary",),
                                             vmem_limit_bytes=_vmem_limit(8 * rows * D_MODEL * 4)),
        name="moe_combine")(pos1, pos2, x1, meta, y)


def _queue_pos_kernel(meta_ref, start_ref, o_ref):
    meta = meta_ref[...]
    lane = lax.broadcasted_iota(jnp.int32, meta.shape, 1).astype(F32)
    start = start_ref[...]
    pos = []
    for e_lane, r_lane in ((_META_E1, _META_R1), (_META_E2, _META_R2)):
        own = jnp.sum(jnp.where(lane == meta[:, e_lane:e_lane + 1], start, 0.0), axis=1, keepdims=True)
        pos.append(own + meta[:, r_lane:r_lane + 1])
    o_ref[...] = jnp.where(lane == 0.0, pos[0], jnp.where(lane == 1.0, pos[1], 0.0))


def _queue_positions(meta, start_row):
    t = meta.shape[0]
    tm = _pick_tile(t, 2048)
    return pl.pallas_call(
        _queue_pos_kernel, grid=(t // tm,),
        in_specs=[pl.BlockSpec((tm, LANES), lambda i: (i, 0)), pl.BlockSpec((1, LANES), lambda i: (0, 0))],
        out_specs=pl.BlockSpec((tm, LANES), lambda i: (i, 0)),
        out_shape=jax.ShapeDtypeStruct((t, LANES), F32),
        compiler_params=pltpu.CompilerParams(dimension_semantics=("arbitrary",)),
        name="moe_queue_pos")(meta, start_row)


def _moe_sorted(x1, tpk, meta, counts, w_gate, w_up, w_down, layer):
    t = x1.shape[0]
    n_rows = 2 * t + N_EXPERTS * _MOE_TILE
    tiles = (counts[0].astype(jnp.int32) + (_MOE_TILE - 1)) // _MOE_TILE
    tile_end = jnp.cumsum(tiles)
    start_row = ((tile_end - tiles) * _MOE_TILE).astype(F32)[None, :]
    tile_ids = jnp.arange(n_rows // _MOE_TILE, dtype=jnp.int32)
    owner = jnp.minimum(jnp.sum((tile_end[None, :N_EXPERTS] <= tile_ids[:, None]).astype(jnp.int32), axis=1),
                        N_EXPERTS - 1)
    n_used = tile_end[N_EXPERTS - 1:N_EXPERTS]
    pos = _queue_positions(meta, start_row)
    pos1 = pos[:, 0].astype(jnp.int32)
    pos2 = pos[:, 1].astype(jnp.int32)
    xs = _scatter_rows(pos1, pos2, tpk, n_rows)
    y = _expert_tiles(owner, n_used, xs, w_gate, w_up, w_down, layer)
    return _combine_rows(pos1, pos2, x1, meta, y)


def _group_mean_matrix(width, groups):
    m = np.zeros((width, width), np.float32)
    for a, b in groups:
        m[a:b, a:b] = 1.0 / (b - a)
    return jnp.asarray(m, BF)


def _pad_heads(wm, n_heads, width, offset=0, stride=None):
    stride = stride or width
    k = wm.shape[0]
    wm = wm.reshape(k, n_heads, stride)[:, :, offset:offset + width]
    return jnp.pad(wm, ((0, 0), (0, 0), (0, HEAD_PAD - width))).reshape(k, n_heads * HEAD_PAD)


def _lane_row(parts):
    return jnp.concatenate([jnp.asarray(p, F32).reshape(-1) for p in parts])[None, :]


def _layer_weights(l, a):
    w = {}
    w_in = a['w_in'][l]
    cuts = np.cumsum([0, Q_RANK, KV_RANK, QK_ROPE, CONV_WIDTH, CONV_WIDTH, MEM_WIDTH])
    c_q, c_kv, k_r, u_v, u_g, q_m = [w_in[:, cuts[i]:cuts[i + 1]] for i in range(6)]
    k_r = jnp.pad(k_r, ((0, 0), (_ROPE0, LANES - _ROPE0 - QK_ROPE)))
    w['win'] = jnp.concatenate([c_q, c_kv, u_v, u_g, q_m, k_r], axis=1).astype(BF)
    w['n1g'] = a['norm1_g'][l][None, :]
    w['qag'] = a['q_a_norm_g'][l][None, :]
    w['kvg'] = a['kv_a_norm_g'][l][None, :]
    w['wuq'] = _pad_heads(a['w_uq'][l], N_HEADS, QK_NOPE + QK_ROPE).astype(BF)
    w_ukv = a['w_ukv'][l]
    w['wuk'] = _pad_heads(w_ukv, N_HEADS, QK_NOPE, 0, QK_NOPE + V_HEAD).astype(BF)
    w['wuv'] = _pad_heads(w_ukv, N_HEADS, V_HEAD, QK_NOPE, QK_NOPE + V_HEAD).astype(BF)
    zeros = lambda n: np.zeros((n,), np.float32)
    w['gq'] = _lane_row([a['qn_nope_g'][l], a['qn_rope_g'][l], zeros(LANES - 96)]) * (MLA_SCALE * LOG2E)
    w['gk'] = _lane_row([a['kn_nope_g'][l], zeros(LANES - QK_NOPE)])
    w['gkr'] = _lane_row([zeros(_ROPE0), a['kn_rope_g'][l], zeros(LANES - 96)])
    qg, kg = [], []
    for hd in range(N_HEADS):
        qg += [(hd * HEAD_PAD, hd * HEAD_PAD + QK_NOPE), (hd * HEAD_PAD + QK_NOPE, hd * HEAD_PAD + 96)]
        kg += [(hd * HEAD_PAD, hd * HEAD_PAD + QK_NOPE)]
    w['grp_q'] = _group_mean_matrix(N_HEADS * HEAD_PAD, qg)
    w['grp_k'] = _group_mean_matrix(N_HEADS * HEAD_PAD, kg)
    w['grp_m'] = _group_mean_matrix(MEM_WIDTH, [(i * MEM_HEAD_DIM, (i + 1) * MEM_HEAD_DIM) for i in range(MEM_HEADS)])
    w['mqg'] = jnp.tile(a['mem_qn_g'][l], MEM_HEADS)[None, :] * MEM_SCALE
    w['mem_kn_g'] = jnp.tile(a['mem_kn_g'][l], MEM_HEADS)[None, :]
    w['mem_norm_g'] = a['mem_norm_g'][l][None, :]
    w['w_mem_k'] = a['w_mem_k'][l].astype(BF)
    w['w_mem_v'] = a['w_mem_v'][l].astype(BF)
    w['conv_w'] = a['conv_w'][l]
    w['conv_b'] = a['conv_b'][l][None, :]
    w['conv_ln_g'] = a['conv_ln_g'][l][None, :]
    w['conv_ln_b'] = a['conv_ln_b'][l][None, :]
    w['w_out'] = a['w_out'][l].astype(BF)
    w['norm2_g'] = a['norm2_g'][l][None, :]
    w_router = jnp.pad(jnp.concatenate([a['w_router_expert'][l], a['w_router_group'][l]], axis=1),
                       ((0, 0), (0, LANES - N_EXPERTS - N_GROUPS)))
    w['w_router_hi'] = w_router.astype(BF)
    w['w_router_lo'] = (w_router - w['w_router_hi'].astype(F32)).astype(BF)
    w['b_router'] = jnp.pad(jnp.concatenate([a['b_router_expert'][l], a['b_router_group'][l]]),
                            (0, LANES - N_EXPERTS - N_GROUPS))[None, :]
    w_uk = w_ukv.reshape(KV_RANK, N_HEADS, QK_NOPE + V_HEAD)[:, :, :QK_NOPE]
    w['wuk_lat'] = jnp.transpose(w_uk, (0, 2, 1)).reshape(KV_RANK, QK_NOPE * N_HEADS).astype(BF)
    w['wuv_lat'] = w_ukv.reshape(KV_RANK, N_HEADS, QK_NOPE + V_HEAD)[:, :, QK_NOPE:].reshape(
        KV_RANK, MLA_WIDTH).astype(BF)
    w['wuk_heads'] = jnp.pad(jnp.transpose(w_uk, (1, 0, 2)), ((0, 0), (0, 0), (0, HEAD_PAD - QK_NOPE)))
    head_of_feat = np.arange(N_HEADS * QK_NOPE) // QK_NOPE
    head_of_qrow = np.arange(_QCOLS) % N_HEADS
    head_of_lane = np.arange(LANES) % N_HEADS
    w['grp_lat'] = jnp.asarray((head_of_qrow[:, None] == head_of_lane[None, :]) / QK_NOPE, BF)
    w['hm_lat'] = jnp.asarray(np.arange(N_HEADS)[:, None] == head_of_feat[None, :], F32)
    return w


def _rope_lane_tables(pos):
    inv_freq = 1.0 / (ROPE_THETA ** (jnp.arange(0, QK_ROPE, 2, dtype=F32) / QK_ROPE))
    ang = pos.astype(F32)[:, None] * inv_freq[None, :]
    cos, sin = jnp.cos(ang), jnp.sin(ang)
    n = pos.shape[0]
    one = jnp.ones((n, _ROPE0), F32)
    zero_hi = jnp.zeros((n, LANES - _ROPE0 - QK_ROPE), F32)
    c = jnp.concatenate([one, cos, cos, zero_hi], axis=1)
    s = jnp.concatenate([jnp.zeros((n, _ROPE0), F32), -sin, sin, zero_hi], axis=1)
    return c, s


def _pick_tile(n, pref):
    return pref if n % pref == 0 else n


def _block_diag_memory(k, v):
    m = k.shape[0]
    hm = (np.arange(MEM_WIDTH)[:, None] // MEM_HEAD_DIM) == (np.arange(MEM_HEADS * m)[None, :] // m)
    kx = jnp.where(hm, jnp.tile(k.T, (1, MEM_HEADS)), 0.0).astype(BF)
    vx = jnp.where(hm.T, jnp.tile(v, (MEM_HEADS, 1)), 0.0).astype(BF)
    return kx, vx


def kernel(x_prompt, x_sample, cache_ckv, cache_kpe, state_conv, cache_mem_k, cache_mem_v, page_table, mem_prompt, norm1_g, w_in, q_a_norm_g, w_uq, kv_a_norm_g, w_ukv, qn_nope_g, qn_rope_g, kn_nope_g, kn_rope_g, conv_w, conv_b, conv_ln_g, conv_ln_b, mem_norm_g, w_mem_k, w_mem_v, mem_qn_g, mem_kn_g, w_out, norm2_g, w_router_group, b_router_group, w_router_expert, b_router_expert, w_gate, w_up, w_down):
    params = dict(norm1_g=norm1_g, w_in=w_in, q_a_norm_g=q_a_norm_g, w_uq=w_uq, kv_a_norm_g=kv_a_norm_g,
                  w_ukv=w_ukv, qn_nope_g=qn_nope_g, qn_rope_g=qn_rope_g, kn_nope_g=kn_nope_g, kn_rope_g=kn_rope_g,
                  conv_w=conv_w, conv_b=conv_b, conv_ln_g=conv_ln_g, conv_ln_b=conv_ln_b, mem_norm_g=mem_norm_g,
                  w_mem_k=w_mem_k, w_mem_v=w_mem_v, mem_qn_g=mem_qn_g, mem_kn_g=mem_kn_g, w_out=w_out,
                  norm2_g=norm2_g, w_router_group=w_router_group, b_router_group=b_router_group,
                  w_router_expert=w_router_expert, b_router_expert=b_router_expert, w_gate=w_gate, w_up=w_up,
                  w_down=w_down)
    depth = w_in.shape[0]
    n_b, n_seq, _ = x_prompt.shape
    assert n_b == 1, "the prompt group is a single sequence"
    n_db, n_dec, _ = x_sample.shape
    n_pages = page_table.shape[1]
    past_len = n_pages * PAGE_SIZE
    n_mem = mem_prompt.shape[1]
    tp = n_b * n_seq
    ts = n_db * n_dec
    assert n_dec <= SUBLANES and n_dec * N_HEADS == _QCOLS

    cos_p, sin_p = _rope_lane_tables(jnp.arange(n_seq))
    cos_s, sin_s = _rope_lane_tables(jnp.tile(past_len + jnp.arange(n_dec), n_db))
    mem_head_mask = jnp.asarray(np.arange(MEM_HEADS)[:, None] == (np.arange(MEM_WIDTH)[None, :] // MEM_HEAD_DIM), F32)
    to_feature_major = lambda c: jnp.transpose(c, (0, 1, 3, 4, 2)).reshape(c.shape[:2] + (MEM_WIDTH, c.shape[2]))
    cache_mk = to_feature_major(cache_mem_k)
    cache_mv = to_feature_major(cache_mem_v)
    cache_kpe_t = jnp.swapaxes(cache_kpe, 2, 3)

    tm_p = _pick_tile(tp, 512)
    tq = _pick_tile(tp, 1024)
    tm_moe = _pick_tile(tp, 1024)
    npg = _PAGES_PER_STEP if n_pages % _PAGES_PER_STEP == 0 else n_pages
    nseq_mem = 8 if n_db % 8 == 0 else n_db

    xp = x_prompt.reshape(tp, D_MODEL)
    xs = x_sample.reshape(ts, D_MODEL)
    outs = {k: [] for k in ('ckv_p', 'kpe_p', 'conv_p', 'memk_p', 'memv_p', 'ckv_s', 'kpe_s', 'conv_s')}
    for l in range(depth):
        w = _layer_weights(l, params)

        mk, mv = _memkv(mem_prompt.reshape(n_mem, D_MODEL), w)
        q, k, v, ckv, kpe, glu, qm = _premix(xp, w, cos_p, sin_p, tm_p)
        a_mla = _flash_attention(q, k, v, tq)
        padded = jnp.concatenate([jnp.zeros((_CONV_HALO, CONV_WIDTH), F32), glu], axis=0)
        a_conv = _conv_rows(padded, w, tm_p)
        kx, vx = _block_diag_memory(mk, mv)
        a_mem = _memattn_rows(qm, kx, vx, tm_p)
        x1, tpk, meta, counts = _outproj(xp, a_mla, a_conv, a_mem, w, tm_p, sort=True)
        xp = _moe_sorted(x1, tpk, meta, counts, w_gate, w_up, w_down, l)
        outs['ckv_p'].append(ckv.reshape(n_b, n_seq, KV_RANK))
        outs['kpe_p'].append(kpe.reshape(n_b, n_seq, QK_ROPE))
        outs['conv_p'].append(padded[-(CONV_K - 1):].reshape(n_b, CONV_K - 1, CONV_WIDTH))
        outs['memk_p'].append(mk.reshape(n_b, n_mem, MEM_HEADS, MEM_HEAD_DIM))
        outs['memv_p'].append(mv.reshape(n_b, n_mem, MEM_HEADS, MEM_HEAD_DIM))

        q, k, v, ckv, kpe, glu, qm = _premix(xs, w, cos_s, sin_s, _pick_tile(ts, 512))
        qlat = _absorb(q, w['gk'], w['wuk_heads'])
        per_seq_rows = lambda a: jnp.transpose(a.reshape(N_HEADS, a.shape[1], n_db, n_dec),
                                               (2, 3, 0, 1)).reshape(n_db, _QCOLS, a.shape[1])
        qa = per_seq_rows(qlat)
        qr = per_seq_rows(q[:, _ROPE0:_ROPE0 + QK_ROPE, :])
        new_ckv = jnp.pad(ckv.reshape(n_db, n_dec, KV_RANK), ((0, 0), (0, 2 * SUBLANES - n_dec), (0, 0)))
        new_kpe_t = jnp.pad(jnp.swapaxes(kpe.reshape(n_db, n_dec, QK_ROPE), 1, 2),
                            ((0, 0), (0, 0), (0, PAGE_SIZE - n_dec)))
        a_mla = _paged_attention(page_table, qa, qr, new_ckv, new_kpe_t,
                                 cache_ckv, cache_kpe_t, l, w, npg, n_dec).reshape(ts, MLA_WIDTH)
        padded = jnp.concatenate([state_conv[l], glu.reshape(n_db, n_dec, CONV_WIDTH)], axis=1)
        a_conv = jnp.transpose(_conv_steps(jnp.transpose(padded, (1, 0, 2)), w), (1, 0, 2)).reshape(ts, CONV_WIDTH)
        q8 = jnp.pad(qm.reshape(n_db, n_dec, MEM_WIDTH), ((0, 0), (0, SUBLANES - n_dec), (0, 0)))
        a_mem = _memattn_seqs(q8, cache_mk, cache_mv, l, mem_head_mask, nseq_mem)[:, :n_dec].reshape(ts, MEM_WIDTH)
        x1, t, gate = _outproj(xs, a_mla, a_conv, a_mem, w, _pick_tile(ts, 512), sort=False)
        xs = _moe(t, gate, x1, w_gate, w_up, w_down, l, _pick_tile(ts, 512))
        outs['ckv_s'].append(ckv.reshape(n_db, n_dec, KV_RANK))
        outs['kpe_s'].append(kpe.reshape(n_db, n_dec, QK_ROPE))
        outs['conv_s'].append(padded[:, -(CONV_K - 1):])

    st = lambda key: jnp.stack(outs[key])
    return (xp.reshape(n_b, n_seq, D_MODEL), xs.reshape(n_db, n_dec, D_MODEL), st('ckv_p'), st('kpe_p'),
            st('conv_p'), st('memk_p'), st('memv_p'), st('ckv_s'), st('kpe_s'), st('conv_s'))
```
